```python
import math
import jax, jax.numpy as jnp
from jax import lax
import numpy as np

D_MODEL = 1024
BATCH = 2
SEQ = 16384
DEPTH = 4

MIX_WIDTH = D_MODEL
S5_WIDTH = D_MODEL // 4
S5_GROUP_DIM = 16
S5_GROUPS = S5_WIDTH // S5_GROUP_DIM
S5_STATE = 64
SSD_INNER = MIX_WIDTH - S5_WIDTH
SSD_HEAD_DIM = 64
SSD_HEADS = SSD_INNER // SSD_HEAD_DIM
SSD_GROUPS = 4
SSD_HPG = SSD_HEADS // SSD_GROUPS
SSD_STATE = 128
SSD_CONV = 4
SSD_CHUNK = 128
SSD_CONV_DIM = SSD_INNER + 2 * SSD_GROUPS * SSD_STATE
IN_COLS = S5_WIDTH + SSD_INNER + SSD_CONV_DIM + SSD_HEADS
MEM_LEN = 256
XA_HEADS = 4
XA_HEAD_DIM = D_MODEL // XA_HEADS
D_FF = 256 * ((8 * D_MODEL // 3 + 255) // 256)
FFN_CONV = 3
NORM_EPS = 1e-6

kernel_name = "hybrid_s5_ssd_xattn_convffn"


def rmsnorm(x, w):
    xf = x.astype(jnp.float32)
    xf = xf * lax.rsqrt(jnp.mean(xf * xf, axis=-1, keepdims=True) + NORM_EPS)
    return (xf * w.astype(jnp.float32)).astype(x.dtype)


def causal_dwconv(x, w, b):
    k = w.shape[0]
    l = x.shape[1]
    xp = jnp.pad(x, ((0, 0), (k - 1, 0), (0, 0)))
    y = xp[:, 0:l] * w[0]
    for i in range(1, k):
        y = y + xp[:, i:i + l] * w[i]
    return y + b


def s5_mixer(u, lam_re, lam_im, log_dt, b_re, b_im, c_re, c_im, d_skip, w_glu):
    bsz, l, _ = u.shape
    f32 = jnp.float32
    uf = u.reshape(bsz, l, S5_GROUPS, S5_GROUP_DIM).astype(f32)
    lr = jnp.minimum(lam_re.astype(f32), -1e-4)
    li = lam_im.astype(f32)
    dt = jnp.exp(log_dt.astype(f32))[:, None]
    mag = jnp.exp(lr * dt)
    ab_re = mag * jnp.cos(li * dt)
    ab_im = mag * jnp.sin(li * dt)
    den = lr * lr + li * li
    f_re = ((ab_re - 1.0) * lr + ab_im * li) / den
    f_im = (ab_im * lr - (ab_re - 1.0) * li) / den
    br = b_re.astype(f32)
    bi = b_im.astype(f32)
    bb_re = f_re[..., None] * br - f_im[..., None] * bi
    bb_im = f_re[..., None] * bi + f_im[..., None] * br
    bu_re = jnp.einsum('blgc,gpc->blgp', uf, bb_re)
    bu_im = jnp.einsum('blgc,gpc->blgp', uf, bb_im)
    a_re = jnp.broadcast_to(ab_re, bu_re.shape)
    a_im = jnp.broadcast_to(ab_im, bu_im.shape)

    def combine(e1, e2):
        a1r, a1i, b1r, b1i = e1
        a2r, a2i, b2r, b2i = e2
        return (a2r * a1r - a2i * a1i,
                a2r * a1i + a2i * a1r,
                a2r * b1r - a2i * b1i + b2r,
                a2r * b1i + a2i * b1r + b2i)

    _, _, h_re, h_im = lax.associative_scan(combine, (a_re, a_im, bu_re, bu_im), axis=1)
    y = (jnp.einsum('blgp,gcp->blgc', h_re, c_re.astype(f32))
         - jnp.einsum('blgp,gcp->blgc', h_im, c_im.astype(f32)))
    y = y + uf * d_skip.astype(f32).reshape(S5_GROUPS, S5_GROUP_DIM)
    y = jax.nn.gelu(y.reshape(bsz, l, S5_WIDTH))
    y = y * jax.nn.sigmoid(y @ w_glu.astype(f32))
    return y.astype(u.dtype)


def ssd_mixer(z, xbc, dt_raw, conv_w, conv_b, dt_bias, a_log, d_skip, norm_w):
    bsz, l, _ = xbc.shape
    f32 = jnp.float32
    nc = l // SSD_CHUNK
    xbc = jax.nn.silu(causal_dwconv(xbc, conv_w, conv_b))
    xs = xbc[..., :SSD_INNER]
    bmat = xbc[..., SSD_INNER:SSD_INNER + SSD_GROUPS * SSD_STATE]
    cmat = xbc[..., SSD_INNER + SSD_GROUPS * SSD_STATE:]
    x = xs.reshape(bsz, nc, SSD_CHUNK, SSD_GROUPS, SSD_HPG, SSD_HEAD_DIM)
    bm = bmat.reshape(bsz, nc, SSD_CHUNK, SSD_GROUPS, SSD_STATE)
    cm = cmat.reshape(bsz, nc, SSD_CHUNK, SSD_GROUPS, SSD_STATE)
    dt = jax.nn.softplus(dt_raw.astype(f32) + dt_bias.astype(f32))
    a = -jnp.exp(a_log.astype(f32))
    dt_c = dt.reshape(bsz, nc, SSD_CHUNK, SSD_GROUPS, SSD_HPG)
    da = jnp.moveaxis(dt_c * a.reshape(SSD_GROUPS, SSD_HPG), 2, -1)
    a_cs = jnp.cumsum(da, axis=-1)
    xdt = x.astype(f32) * dt_c[..., None]
    seg = a_cs[..., :, None] - a_cs[..., None, :]
    causal = jnp.tril(jnp.ones((SSD_CHUNK, SSD_CHUNK), dtype=bool))
    lmat = jnp.exp(jnp.where(causal, seg, -jnp.inf))
    cb = jnp.einsum('bclgn,bcsgn->bcgls', cm.astype(f32), bm.astype(f32))
    y_diag = jnp.einsum('bcgls,bcghls,bcsghp->bclghp', cb, lmat, xdt)
    decay_states = jnp.exp(a_cs[..., -1:] - a_cs)
    states = jnp.einsum('bclgn,bcghl,bclghp->bcghpn', bm.astype(f32), decay_states, xdt)
    chunk_decay = jnp.exp(a_cs[..., -1])

    def step(carry, inp):
        st, dec = inp
        return carry * dec[..., None, None] + st, carry

    init = jnp.zeros((bsz, SSD_GROUPS, SSD_HPG, SSD_HEAD_DIM, SSD_STATE), f32)
    _, prev = lax.scan(step, init, (jnp.moveaxis(states, 1, 0), jnp.moveaxis(chunk_decay, 1, 0)))
    prev = jnp.moveaxis(prev, 0, 1)
    y_off = jnp.einsum('bclgn,bcghpn,bcghl->bclghp', cm.astype(f32), prev, jnp.exp(a_cs))
    y = y_diag + y_off + x.astype(f32) * d_skip.astype(f32).reshape(SSD_GROUPS, SSD_HPG)[:, :, None]
    y = y.reshape(bsz, l, SSD_INNER).astype(z.dtype)
    return rmsnorm(y * jax.nn.silu(z), norm_w)


def cross_attention(h, mem_n, wq, wk, wv, wo):
    bsz, l, _ = h.shape
    m = mem_n.shape[1]
    q = (h @ wq).reshape(bsz, l, XA_HEADS, XA_HEAD_DIM)
    k = (mem_n @ wk).reshape(bsz, m, XA_HEADS, XA_HEAD_DIM)
    v = (mem_n @ wv).reshape(bsz, m, XA_HEADS, XA_HEAD_DIM)
    s = jnp.einsum('blhd,bmhd->bhlm', q, k).astype(jnp.float32) * (XA_HEAD_DIM ** -0.5)
    p = jax.nn.softmax(s, axis=-1).astype(v.dtype)
    o = jnp.einsum('bhlm,bmhd->blhd', p, v).reshape(bsz, l, D_MODEL)
    return o @ wo


def conv_ffn(h, w_up, conv_w, conv_b, w_down):
    gv = h @ w_up
    g = causal_dwconv(gv[..., :D_FF], conv_w, conv_b)
    v = gv[..., D_FF:]
    return (jax.nn.silu(g) * v) @ w_down


def setup_inputs(seed: int = 0) -> dict:
    key = jax.random.key(seed)
    ks = jax.random.split(key, 32)
    f32 = jnp.float32

    def nrm(k, shape, scale):
        return jax.random.normal(k, shape, f32) * scale

    def gain(k, shape):
        return 1.0 + 0.02 * jax.random.normal(k, shape, f32)

    L = DEPTH
    x = jax.random.normal(ks[0], (BATCH, SEQ, D_MODEL), f32)
    mem = jax.random.normal(ks[1], (BATCH, MEM_LEN, D_MODEL), f32)
    mix_norm_w = gain(ks[2], (L, D_MODEL))
    w_in = nrm(ks[3], (L, D_MODEL, IN_COLS), D_MODEL ** -0.5)
    n_idx = jnp.arange(S5_STATE, dtype=f32)
    s5_lambda_re = -0.5 + 0.01 * jax.random.normal(ks[4], (L, S5_GROUPS, S5_STATE), f32)
    s5_lambda_im = math.pi * n_idx + 0.01 * jax.random.normal(ks[5], (L, S5_GROUPS, S5_STATE), f32)
    s5_log_dt = jax.random.uniform(ks[6], (L, S5_GROUPS), f32, math.log(0.001), math.log(0.1))
    s5_b_re = nrm(ks[7], (L, S5_GROUPS, S5_STATE, S5_GROUP_DIM), (2 * S5_GROUP_DIM) ** -0.5)
    s5_b_im = nrm(ks[8], (L, S5_GROUPS, S5_STATE, S5_GROUP_DIM), (2 * S5_GROUP_DIM) ** -0.5)
    s5_c_re = nrm(ks[9], (L, S5_GROUPS, S5_GROUP_DIM, S5_STATE), S5_STATE ** -0.5)
    s5_c_im = nrm(ks[10], (L, S5_GROUPS, S5_GROUP_DIM, S5_STATE), S5_STATE ** -0.5)
    s5_d = nrm(ks[11], (L, S5_WIDTH), 1.0)
    s5_w_glu = nrm(ks[12], (L, S5_WIDTH, S5_WIDTH), S5_WIDTH ** -0.5)
    ssd_conv_w = nrm(ks[13], (L, SSD_CONV, SSD_CONV_DIM), SSD_CONV ** -0.5)
    ssd_conv_b = nrm(ks[14], (L, SSD_CONV_DIM), 0.01)
    dt0 = jnp.exp(jax.random.uniform(ks[15], (L, SSD_HEADS), f32, math.log(0.001), math.log(0.1)))
    ssd_dt_bias = dt0 + jnp.log(-jnp.expm1(-dt0))
    ssd_a_log = jnp.log(jax.random.uniform(ks[16], (L, SSD_HEADS), f32, 1.0, 16.0))
    ssd_d = gain(ks[17], (L, SSD_HEADS))
    ssd_norm_w = gain(ks[18], (L, SSD_INNER))
    w_out = nrm(ks[19], (L, MIX_WIDTH, D_MODEL), MIX_WIDTH ** -0.5)
    xa_norm_w = gain(ks[20], (L, D_MODEL))
    mem_norm_w = gain(ks[21], (L, D_MODEL))
    xa_wq = nrm(ks[22], (L, D_MODEL, D_MODEL), D_MODEL ** -0.5)
    xa_wk = nrm(ks[23], (L, D_MODEL, D_MODEL), D_MODEL ** -0.5)
    xa_wv = nrm(ks[24], (L, D_MODEL, D_MODEL), D_MODEL ** -0.5)
    xa_wo = nrm(ks[25], (L, D_MODEL, D_MODEL), D_MODEL ** -0.5)
    ffn_norm_w = gain(ks[26], (L, D_MODEL))
    ffn_w_up = nrm(ks[27], (L, D_MODEL, 2 * D_FF), D_MODEL ** -0.5)
    ffn_conv_w = nrm(ks[28], (L, FFN_CONV, D_FF), FFN_CONV ** -0.5)
    ffn_conv_b = nrm(ks[29], (L, D_FF), 0.01)
    ffn_w_down = nrm(ks[30], (L, D_FF, D_MODEL), D_FF ** -0.5)
    final_norm_w = gain(ks[31], (D_MODEL,))
    return {"x": x, "mem": mem, "mix_norm_w": mix_norm_w, "w_in": w_in,
            "s5_lambda_re": s5_lambda_re, "s5_lambda_im": s5_lambda_im, "s5_log_dt": s5_log_dt,
            "s5_b_re": s5_b_re, "s5_b_im": s5_b_im, "s5_c_re": s5_c_re, "s5_c_im": s5_c_im,
            "s5_d": s5_d, "s5_w_glu": s5_w_glu,
            "ssd_conv_w": ssd_conv_w, "ssd_conv_b": ssd_conv_b, "ssd_dt_bias": ssd_dt_bias,
            "ssd_a_log": ssd_a_log, "ssd_d": ssd_d, "ssd_norm_w": ssd_norm_w, "w_out": w_out,
            "xa_norm_w": xa_norm_w, "mem_norm_w": mem_norm_w, "xa_wq": xa_wq, "xa_wk": xa_wk,
            "xa_wv": xa_wv, "xa_wo": xa_wo, "ffn_norm_w": ffn_norm_w, "ffn_w_up": ffn_w_up,
            "ffn_conv_w": ffn_conv_w, "ffn_conv_b": ffn_conv_b, "ffn_w_down": ffn_w_down,
            "final_norm_w": final_norm_w}


def reference(x, mem, mix_norm_w, w_in, s5_lambda_re, s5_lambda_im, s5_log_dt, s5_b_re, s5_b_im,
              s5_c_re, s5_c_im, s5_d, s5_w_glu, ssd_conv_w, ssd_conv_b, ssd_dt_bias, ssd_a_log,
              ssd_d, ssd_norm_w, w_out, xa_norm_w, mem_norm_w, xa_wq, xa_wk, xa_wv, xa_wo,
              ffn_norm_w, ffn_w_up, ffn_conv_w, ffn_conv_b, ffn_w_down, final_norm_w):
    c_u = S5_WIDTH
    c_z = c_u + SSD_INNER
    c_xbc = c_z + SSD_CONV_DIM
    for i in range(DEPTH):
        h = rmsnorm(x, mix_norm_w[i])
        proj = h @ w_in[i]
        y_s5 = s5_mixer(proj[..., :c_u], s5_lambda_re[i], s5_lambda_im[i], s5_log_dt[i],
                        s5_b_re[i], s5_b_im[i], s5_c_re[i], s5_c_im[i], s5_d[i], s5_w_glu[i])
        y_ssd = ssd_mixer(proj[..., c_u:c_z], proj[..., c_z:c_xbc], proj[..., c_xbc:],
                          ssd_conv_w[i], ssd_conv_b[i], ssd_dt_bias[i], ssd_a_log[i],
                          ssd_d[i], ssd_norm_w[i])
        x = x + jnp.concatenate([y_s5, y_ssd], axis=-1) @ w_out[i]
        h = rmsnorm(x, xa_norm_w[i])
        mem_n = rmsnorm(mem, mem_norm_w[i])
        x = x + cross_attention(h, mem_n, xa_wq[i], xa_wk[i], xa_wv[i], xa_wo[i])
        h = rmsnorm(x, ffn_norm_w[i])
        x = x + conv_ffn(h, ffn_w_up[i], ffn_conv_w[i], ffn_conv_b[i], ffn_w_down[i])
    return rmsnorm(x, final_norm_w)
```

```python
import functools
import math

import jax
import jax.numpy as jnp
from jax import lax
from jax.experimental import pallas as pl
from jax.experimental.pallas import tpu as pltpu

F32 = jnp.float32
BF16 = jnp.bfloat16

NORM_EPS = 1e-6
LANES = 128
SUBLANES = 8
VMEM_LIMIT = 56 * 1024 * 1024

S5_GROUP_DIM = 16
S5_STATE = 64
SSD_HEAD_DIM = 64
SSD_GROUPS = 4
SSD_STATE = 128
SSD_CONV = 4
SSD_CHUNK = 128
XA_HEADS = 4
FFN_CONV = 3


def _cparams(n_axes, sequential):
    sem = ("arbitrary" if sequential else "parallel",) * n_axes
    return pltpu.CompilerParams(dimension_semantics=sem, vmem_limit_bytes=VMEM_LIMIT)


def _const_spec(shape):
    nd = len(shape)
    return pl.BlockSpec(shape, lambda *_: (0,) * nd)


def _rms(xf, w):
    ms = jnp.mean(xf * xf, axis=-1, keepdims=True)
    return xf * lax.rsqrt(ms + NORM_EPS) * w


def _dot(a, b):
    return jnp.dot(a, b, preferred_element_type=F32)


def _split3(x):
    hi = x.astype(BF16)
    r1 = x - hi.astype(F32)
    mid = r1.astype(BF16)
    lo = (r1 - mid.astype(F32)).astype(BF16)
    return hi, mid, lo


def _proj_kernel(x_ref, nw_ref, w_ref, u_ref, z_ref, xbc_ref, dt_ref, *, c_u, c_z, c_xbc):
    h = _rms(x_ref[...], nw_ref[...]).astype(BF16)
    u_ref[...] = _dot(h, w_ref[:, 0:c_u])
    z_ref[...] = _dot(h, w_ref[:, c_u:c_z])
    xbc_ref[...] = _dot(h, w_ref[:, c_z:c_xbc])
    dt_ref[...] = _dot(h, w_ref[:, c_xbc:c_xbc + LANES])


def _in_proj(x2, norm_w, w_in_p, c_u, c_z, c_xbc, tm):
    t, d = x2.shape
    kern = functools.partial(_proj_kernel, c_u=c_u, c_z=c_z, c_xbc=c_xbc)
    row = lambda w: pl.BlockSpec((tm, w), lambda i: (i, 0))
    return pl.pallas_call(
        kern,
        grid=(t // tm,),
        in_specs=[row(d), _const_spec(norm_w.shape), _const_spec(w_in_p.shape)],
        out_specs=[row(c_u), row(c_z - c_u), row(c_xbc - c_z), row(LANES)],
        out_shape=[jax.ShapeDtypeStruct((t, c_u), F32),
                   jax.ShapeDtypeStruct((t, c_z - c_u), F32),
                   jax.ShapeDtypeStruct((t, c_xbc - c_z), F32),
                   jax.ShapeDtypeStruct((t, LANES), F32)],
        compiler_params=_cparams(1, False),
        name="in_proj",
    )(x2, norm_w, w_in_p)


def _s5_kernel(u_ref, bblk_ref, lampow_ref, cblk_ref, d_ref, wglu_ref, y_ref,
               bu_ref, h_ref, carry_ref, cin_ref, *, seg_len, n_state):
    r_len = seg_len
    ns = n_state

    @pl.when(pl.program_id(1) == 0)
    def _():
        carry_ref[...] = jnp.zeros_like(carry_ref)

    nt = ns // LANES
    u = u_ref[...]
    bu = _dot(u.astype(BF16), bblk_ref[...])
    for j in range(2 * nt):
        bu_ref[j] = bu[:, j * LANES:(j + 1) * LANES]

    def lanes(ref, r0, r1, j):
        return ref[r0:r1, j * LANES:(j + 1) * LANES]

    a_re = [jnp.broadcast_to(lanes(lampow_ref, 0, 1, j), (SUBLANES, LANES)) for j in range(nt)]
    a_im = [jnp.broadcast_to(lanes(lampow_ref, 0, 1, nt + j), (SUBLANES, LANES))
            for j in range(nt)]

    h_re = [jnp.zeros((SUBLANES, LANES), F32) for _ in range(nt)]
    h_im = [jnp.zeros((SUBLANES, LANES), F32) for _ in range(nt)]
    for r in range(r_len):
        rows = pl.ds(r, SUBLANES, stride=r_len)
        for j in range(nt):
            n_re = a_re[j] * h_re[j] - a_im[j] * h_im[j] + bu_ref[j, rows, :]
            n_im = a_re[j] * h_im[j] + a_im[j] * h_re[j] + bu_ref[nt + j, rows, :]
            h_re[j], h_im[j] = n_re, n_im
            h_ref[j, rows, :] = n_re
            h_ref[nt + j, rows, :] = n_im

    for j in range(nt):
        q_re = lanes(lampow_ref, r_len - 1, r_len, j)
        q_im = lanes(lampow_ref, r_len - 1, r_len, nt + j)
        c_re = lanes(carry_ref, 0, 1, j)
        c_im = lanes(carry_ref, 0, 1, nt + j)
        for s in range(SUBLANES):
            cin_ref[s:s + 1, j * LANES:(j + 1) * LANES] = c_re
            cin_ref[s:s + 1, (nt + j) * LANES:(nt + j + 1) * LANES] = c_im
            e_re = h_re[j][s:s + 1, :]
            e_im = h_im[j][s:s + 1, :]
            c_re, c_im = (e_re + q_re * c_re - q_im * c_im,
                          e_im + q_re * c_im + q_im * c_re)
        carry_ref[0:1, j * LANES:(j + 1) * LANES] = c_re
        carry_ref[0:1, (nt + j) * LANES:(nt + j + 1) * LANES] = c_im

    ci_re = [lanes(cin_ref, 0, SUBLANES, j) for j in range(nt)]
    ci_im = [lanes(cin_ref, 0, SUBLANES, nt + j) for j in range(nt)]
    for r in range(r_len):
        rows = pl.ds(r, SUBLANES, stride=r_len)
        for j in range(nt):
            p_re = lanes(lampow_ref, r, r + 1, j)
            p_im = lanes(lampow_ref, r, r + 1, nt + j)
            h_ref[j, rows, :] = h_ref[j, rows, :] + (p_re * ci_re[j] - p_im * ci_im[j])
            h_ref[nt + j, rows, :] = (h_ref[nt + j, rows, :]
                                      + (p_re * ci_im[j] + p_im * ci_re[j]))

    h_all = jnp.concatenate([h_ref[j].astype(BF16) for j in range(2 * nt)], axis=1)
    y = _dot(h_all, cblk_ref[...]) + u * d_ref[...]
    y = jax.nn.gelu(y)
    gate = jax.nn.sigmoid(_dot(y.astype(BF16), wglu_ref[...]))
    y_ref[...] = (y * gate).astype(y_ref.dtype)


def _s5_mixer(u3, bblk, lampow, cblk, d_skip, w_glu, tm):
    b, l, w = u3.shape
    ns = bblk.shape[1] // 2
    seg_len = tm // SUBLANES
    kern = functools.partial(_s5_kernel, seg_len=seg_len, n_state=ns)
    return pl.pallas_call(
        kern,
        grid=(b, l // tm),
        in_specs=[pl.BlockSpec((None, tm, w), lambda bi, i: (bi, i, 0)),
                  _const_spec(bblk.shape), _const_spec(lampow.shape),
                  _const_spec(cblk.shape), _const_spec(d_skip.shape),
                  _const_spec(w_glu.shape)],
        out_specs=pl.BlockSpec((None, tm, w), lambda bi, i: (bi, i, 0)),
        out_shape=jax.ShapeDtypeStruct((b, l, w), BF16),
        scratch_shapes=[pltpu.VMEM((2 * ns // LANES, tm, LANES), F32),
                        pltpu.VMEM((2 * ns // LANES, tm, LANES), F32),
                        pltpu.VMEM((SUBLANES, 2 * ns), F32),
                        pltpu.VMEM((SUBLANES, 2 * ns), F32)],
        compiler_params=_cparams(2, True),
        name="s5_mixer",
    )(u3, bblk, lampow, cblk, d_skip, w_glu)


def _ssd_kernel(z_ref, xbc_ref, dt_ref, cw_ref, cb_ref, dtb_ref, alog_ref, dsk_ref,
                nw_ref, tri_ref, y_ref, cat_ref, state_ref, ybuf_ref,
                *, n_heads, inner):
    tm = z_ref.shape[0]
    q = SSD_CHUNK
    hd = SSD_HEAD_DIM
    hpg = n_heads // SSD_GROUPS
    halo = SUBLANES
    gs = SSD_GROUPS * SSD_STATE

    @pl.when(pl.program_id(1) == 0)
    def _():
        cat_ref[0:halo, :] = jnp.zeros((halo, cat_ref.shape[1]), F32)
        state_ref[...] = jnp.zeros_like(state_ref)

    cat_ref[halo:halo + tm, :] = xbc_ref[...]
    acc = cb_ref[...] + cw_ref[SSD_CONV - 1:SSD_CONV, :] * cat_ref[halo:halo + tm, :]
    for k in range(1, SSD_CONV):
        acc = acc + (cw_ref[SSD_CONV - 1 - k:SSD_CONV - k, :]
                     * cat_ref[pl.ds(halo - k, tm), :])
    cat_ref[0:halo, :] = cat_ref[tm:tm + halo, :]
    xbc = acc * jax.nn.sigmoid(acc)

    dtv = jax.nn.softplus(dt_ref[...] + dtb_ref[...])
    a_neg = -jnp.exp(alog_ref[...])
    da = dtv * a_neg
    tri = tri_ref[...]
    li = lax.broadcasted_iota(jnp.int32, (q, q), 0)
    si = lax.broadcasted_iota(jnp.int32, (q, q), 1)
    causal = li >= si

    for c in range(tm // q):
        r0 = c * q
        hi, mid, lo = _split3(da[r0:r0 + q, :])
        a_cs = _dot(tri, hi) + _dot(tri, mid) + _dot(tri, lo)
        a_cs_t = a_cs.T
        dt_c = dtv[r0:r0 + q, :]
        dt_t = dt_c.T
        for g in range(SSD_GROUPS):
            b_g = xbc[r0:r0 + q, inner + g * SSD_STATE:inner + (g + 1) * SSD_STATE]
            c_g = xbc[r0:r0 + q, inner + gs + g * SSD_STATE:inner + gs + (g + 1) * SSD_STATE]
            b_bf = b_g.astype(BF16)
            c_bf = c_g.astype(BF16)
            cb = lax.dot_general(c_bf, b_bf, (((1,), (1,)), ((), ())),
                                 preferred_element_type=F32)
            for j in range(hpg):
                h = g * hpg + j
                x_h = xbc[r0:r0 + q, h * hd:(h + 1) * hd]
                col = a_cs[:, h:h + 1]
                row = a_cs_t[h:h + 1, :]
                lmat = jnp.exp(jnp.where(causal, col - row, -jnp.inf))
                m = (cb * lmat * dt_t[h:h + 1, :]).astype(BF16)
                y_h = _dot(m, x_h.astype(BF16))
                s_prev = state_ref[h]
                y_h = y_h + jnp.exp(col) * _dot(c_bf, s_prev.astype(BF16))
                a_last = a_cs[q - 1:q, h:h + 1]
                wgt = jnp.exp(a_last - col) * dt_c[:, h:h + 1]
                xw = (x_h * wgt).astype(BF16)
                upd = lax.dot_general(b_bf, xw, (((0,), (0,)), ((), ())),
                                      preferred_element_type=F32)
                state_ref[h] = s_prev * jnp.exp(a_last) + upd
                y_h = y_h + x_h * dsk_ref[0:1, h:h + 1]
                ybuf_ref[r0:r0 + q, h * hd:(h + 1) * hd] = y_h

    zz = z_ref[...]
    gated = ybuf_ref[...] * (zz * jax.nn.sigmoid(zz))
    y_ref[...] = _rms(gated, nw_ref[...]).astype(y_ref.dtype)


def _ssd_mixer(z3, xbc3, dt3, conv_w, conv_b, dt_bias, a_log, d_skip, norm_w, tri, n_heads, tm):
    b, l, inner = z3.shape
    cdim = xbc3.shape[2]
    kern = functools.partial(_ssd_kernel, n_heads=n_heads, inner=inner)
    blk = lambda w: pl.BlockSpec((None, tm, w), lambda bi, i: (bi, i, 0))
    return pl.pallas_call(
        kern,
        grid=(b, l // tm),
        in_specs=[blk(inner), blk(cdim), blk(LANES),
                  _const_spec(conv_w.shape), _const_spec(conv_b.shape),
                  _const_spec(dt_bias.shape), _const_spec(a_log.shape),
                  _const_spec(d_skip.shape), _const_spec(norm_w.shape),
                  _const_spec(tri.shape)],
        out_specs=blk(inner),
        out_shape=jax.ShapeDtypeStruct((b, l, inner), BF16),
        scratch_shapes=[pltpu.VMEM((tm + SUBLANES, cdim), F32),
                        pltpu.VMEM((n_heads, SSD_STATE, SSD_HEAD_DIM), F32),
                        pltpu.VMEM((tm, inner), F32)],
        compiler_params=_cparams(2, True),
        name="ssd_mixer",
    )(z3, xbc3, dt3, conv_w, conv_b, dt_bias, a_log, d_skip, norm_w, tri)


def _kv_kernel(mem_ref, nw_ref, wk_ref, wv_ref, k_ref, v_ref):
    m = _rms(mem_ref[...], nw_ref[...]).astype(BF16)
    k_ref[...] = _dot(m, wk_ref[...]).astype(k_ref.dtype)
    v_ref[...] = _dot(m, wv_ref[...]).astype(v_ref.dtype)


def _mem_kv(mem, mem_norm_w, wk, wv):
    b, m, d = mem.shape
    nl = wk.shape[0]
    return pl.pallas_call(
        _kv_kernel,
        grid=(nl, b),
        in_specs=[pl.BlockSpec((None, m, d), lambda li, bi: (bi, 0, 0)),
                  pl.BlockSpec((None, 1, d), lambda li, bi: (li, 0, 0)),
                  pl.BlockSpec((None, d, d), lambda li, bi: (li, 0, 0)),
                  pl.BlockSpec((None, d, d), lambda li, bi: (li, 0, 0))],
        out_specs=[pl.BlockSpec((None, None, m, d), lambda li, bi: (li, bi, 0, 0)),
                   pl.BlockSpec((None, None, m, d), lambda li, bi: (li, bi, 0, 0))],
        out_shape=[jax.ShapeDtypeStruct((nl, b, m, d), BF16),
                   jax.ShapeDtypeStruct((nl, b, m, d), BF16)],
        compiler_params=_cparams(2, False),
        name="mem_kv",
    )(mem, mem_norm_w, wk, wv)


def _xattn_kernel(x_ref, ys5_ref, yssd_ref, wo1_ref, wo2_ref, nw_ref, wq_ref, k_ref, v_ref,
                  wo_ref, o_ref, att_ref):
    x1 = x_ref[...] + _dot(ys5_ref[...], wo1_ref[...]) + _dot(yssd_ref[...], wo2_ref[...])
    h = _rms(x1, nw_ref[...]).astype(BF16)
    d = x1.shape[1]
    hd = d // XA_HEADS
    scale = hd ** -0.5
    for hh in range(XA_HEADS):
        cols = slice(hh * hd, (hh + 1) * hd)
        q = _dot(h, wq_ref[:, cols]).astype(BF16)
        s = lax.dot_general(q, k_ref[:, cols], (((1,), (1,)), ((), ())),
                            preferred_element_type=F32) * scale
        s = s - jnp.max(s, axis=-1, keepdims=True)
        p = jnp.exp(s)
        p = p / jnp.sum(p, axis=-1, keepdims=True)
        att_ref[:, cols] = _dot(p.astype(BF16), v_ref[:, cols]).astype(att_ref.dtype)
    o_ref[...] = x1 + _dot(att_ref[...], wo_ref[...])


def _xattn(x3, ys5, yssd, wo1, wo2, norm_w, wq, k, v, wo, tm):
    b, l, d = x3.shape
    m = k.shape[1]
    blk = lambda w: pl.BlockSpec((None, tm, w), lambda bi, i: (bi, i, 0))
    kvspec = pl.BlockSpec((None, m, d), lambda bi, i: (bi, 0, 0))
    return pl.pallas_call(
        _xattn_kernel,
        grid=(b, l // tm),
        in_specs=[blk(d), blk(ys5.shape[2]), blk(yssd.shape[2]),
                  _const_spec(wo1.shape), _const_spec(wo2.shape), _const_spec(norm_w.shape),
                  _const_spec(wq.shape), kvspec, kvspec, _const_spec(wo.shape)],
        out_specs=blk(d),
        out_shape=jax.ShapeDtypeStruct((b, l, d), F32),
        scratch_shapes=[pltpu.VMEM((tm, d), BF16)],
        compiler_params=_cparams(2, False),
        name="xattn",
    )(x3, ys5, yssd, wo1, wo2, norm_w, wq, k, v, wo)


def _ffn_kernel(x_ref, nw_ref, wup_ref, cw_ref, cb_ref, wdn_ref, fnw_ref, o_ref,
                halo_ref, acc_ref, *, d_ff, chunk, final_norm):
    tm = x_ref.shape[0]

    @pl.when(pl.program_id(1) == 0)
    def _():
        halo_ref[...] = jnp.zeros_like(halo_ref)

    x = x_ref[...]
    h = _rms(x, nw_ref[...]).astype(BF16)
    row = lax.broadcasted_iota(jnp.int32, (tm, chunk), 0)
    for c in range(d_ff // chunk):
        cols = slice(c * chunk, (c + 1) * chunk)
        g = _dot(h, wup_ref[:, cols])
        v = _dot(h, wup_ref[:, d_ff + c * chunk:d_ff + (c + 1) * chunk])
        p1 = halo_ref[SUBLANES - 1:SUBLANES, cols]
        p2 = halo_ref[SUBLANES - 2:SUBLANES - 1, cols]
        gm1 = jnp.where(row == 0, p1, pltpu.roll(g, 1, 0))
        gm2 = jnp.where(row == 0, p2, jnp.where(row == 1, p1, pltpu.roll(g, 2, 0)))
        halo_ref[:, cols] = g[tm - SUBLANES:tm, :]
        y = (cb_ref[0:1, cols] + cw_ref[0:1, cols] * gm2 + cw_ref[1:2, cols] * gm1
             + cw_ref[2:3, cols] * g)
        act = (y * jax.nn.sigmoid(y) * v).astype(BF16)
        contrib = _dot(act, wdn_ref[cols, :])
        if c == 0:
            acc_ref[...] = contrib
        else:
            acc_ref[...] += contrib
    out = x + acc_ref[...]
    if final_norm:
        out = _rms(out, fnw_ref[...])
    o_ref[...] = out


def _ffn(x3, norm_w, w_up, conv_w, conv_b, w_down, final_w, final_norm, tm, chunk):
    b, l, d = x3.shape
    d_ff = w_down.shape[0]
    kern = functools.partial(_ffn_kernel, d_ff=d_ff, chunk=chunk, final_norm=final_norm)
    blk = pl.BlockSpec((None, tm, d), lambda bi, i: (bi, i, 0))
    return pl.pallas_call(
        kern,
        grid=(b, l // tm),
        in_specs=[blk, _const_spec(norm_w.shape), _const_spec(w_up.shape),
                  _const_spec(conv_w.shape), _const_spec(conv_b.shape),
                  _const_spec(w_down.shape), _const_spec(final_w.shape)],
        out_specs=blk,
        out_shape=jax.ShapeDtypeStruct((b, l, d), F32),
        scratch_shapes=[pltpu.VMEM((SUBLANES, d_ff), F32),
                        pltpu.VMEM((tm, d), F32)],
        compiler_params=_cparams(2, True),
        name="conv_ffn",
    )(x3, norm_w, w_up, conv_w, conv_b, w_down, final_w)


def _s5_params(lam_re, lam_im, log_dt, b_re, b_im, c_re, c_im, seg_len):
    g, p = lam_re.shape
    cg = b_re.shape[2]
    lr = jnp.minimum(lam_re.astype(F32), -1e-4)
    li = lam_im.astype(F32)
    dt = jnp.exp(log_dt.astype(F32))[:, None]
    mag = jnp.exp(lr * dt)
    ab_re = mag * jnp.cos(li * dt)
    ab_im = mag * jnp.sin(li * dt)
    den = lr * lr + li * li
    f_re = ((ab_re - 1.0) * lr + ab_im * li) / den
    f_im = (ab_im * lr - (ab_re - 1.0) * li) / den
    br = b_re.astype(F32)
    bi = b_im.astype(F32)
    bb_re = f_re[..., None] * br - f_im[..., None] * bi
    bb_im = f_re[..., None] * bi + f_im[..., None] * br
    eye = jnp.eye(g, dtype=F32)

    def in_blk(bb):
        return jnp.einsum('gpc,gh->gchp', bb, eye).reshape(g * cg, g * p)

    def out_blk(cc):
        return jnp.einsum('gcp,gh->gphc', cc, eye).reshape(g * p, g * cg)

    bblk = jnp.concatenate([in_blk(bb_re), in_blk(bb_im)], axis=1).astype(BF16)
    cblk = jnp.concatenate([out_blk(c_re.astype(F32)), -out_blk(c_im.astype(F32))],
                           axis=0).astype(BF16)
    steps = jnp.arange(1, seg_len + 1, dtype=F32)[:, None]
    lam_log = (lr * dt).reshape(1, g * p)
    lam_ang = (li * dt).reshape(1, g * p)
    pmag = jnp.exp(steps * lam_log)
    lampow = jnp.concatenate([pmag * jnp.cos(steps * lam_ang),
                              pmag * jnp.sin(steps * lam_ang)], axis=1)
    lampow = lampow.at[0].set(jnp.concatenate([ab_re.reshape(-1), ab_im.reshape(-1)]))
    return bblk, lampow, cblk


def _pad_lanes(v, width):
    v = v.reshape(1, -1).astype(F32)
    return jnp.pad(v, ((0, 0), (0, width - v.shape[1])))


TM_PROJ = 512
TM_S5 = 512
TM_SSD = 256
TM_XA = 512
TM_FFN = 512
FFN_CHUNK = 256


def kernel(x, mem, mix_norm_w, w_in, s5_lambda_re, s5_lambda_im, s5_log_dt, s5_b_re, s5_b_im,
           s5_c_re, s5_c_im, s5_d, s5_w_glu, ssd_conv_w, ssd_conv_b, ssd_dt_bias, ssd_a_log,
           ssd_d, ssd_norm_w, w_out, xa_norm_w, mem_norm_w, xa_wq, xa_wk, xa_wv, xa_wo,
           ffn_norm_w, ffn_w_up, ffn_conv_w, ffn_conv_b, ffn_w_down, final_norm_w):
    bsz, seq, d = x.shape
    depth = w_in.shape[0]
    s5_w = s5_d.shape[1]
    inner = ssd_norm_w.shape[1]
    n_heads = ssd_a_log.shape[1]
    cdim = ssd_conv_w.shape[2]
    c_u, c_z, c_xbc = s5_w, s5_w + inner, s5_w + inner + cdim
    in_cols = w_in.shape[2]
    t = bsz * seq

    tri = jnp.tril(jnp.ones((SSD_CHUNK, SSD_CHUNK), F32)).astype(BF16)
    k_all, v_all = _mem_kv(mem, mem_norm_w.reshape(depth, 1, d),
                           xa_wk.astype(BF16), xa_wv.astype(BF16))

    for i in range(depth):
        w_in_p = jnp.pad(w_in[i], ((0, 0), (0, c_xbc + LANES - in_cols))).astype(BF16)
        u, z, xbc, dtr = _in_proj(x.reshape(t, d), mix_norm_w[i].reshape(1, d), w_in_p,
                                  c_u, c_z, c_xbc, TM_PROJ)
        bblk, lampow, cblk = _s5_params(s5_lambda_re[i], s5_lambda_im[i], s5_log_dt[i],
                                        s5_b_re[i], s5_b_im[i], s5_c_re[i], s5_c_im[i],
                                        TM_S5 // SUBLANES)
        y_s5 = _s5_mixer(u.reshape(bsz, seq, c_u), bblk, lampow, cblk,
                         s5_d[i].reshape(1, s5_w).astype(F32), s5_w_glu[i].astype(BF16), TM_S5)
        y_ssd = _ssd_mixer(z.reshape(bsz, seq, inner), xbc.reshape(bsz, seq, cdim),
                           dtr.reshape(bsz, seq, LANES),
                           ssd_conv_w[i].astype(F32), ssd_conv_b[i].reshape(1, cdim).astype(F32),
                           _pad_lanes(ssd_dt_bias[i], LANES), _pad_lanes(ssd_a_log[i], LANES),
                           _pad_lanes(ssd_d[i], LANES), ssd_norm_w[i].reshape(1, inner).astype(F32),
                           tri, n_heads, TM_SSD)
        wo_bf = w_out[i].astype(BF16)
        x = _xattn(x, y_s5, y_ssd, wo_bf[:s5_w], wo_bf[s5_w:], xa_norm_w[i].reshape(1, d),
                   xa_wq[i].astype(BF16), k_all[i], v_all[i], xa_wo[i].astype(BF16), TM_XA)
        x = _ffn(x, ffn_norm_w[i].reshape(1, d), ffn_w_up[i].astype(BF16),
                 ffn_conv_w[i].astype(F32), ffn_conv_b[i].reshape(1, -1).astype(F32),
                 ffn_w_down[i].astype(BF16), final_norm_w.reshape(1, d),
                 i == depth - 1, TM_FFN, FFN_CHUNK)
    return x
```

```python
import functools
import math

import jax
import jax.numpy as jnp
from jax import lax
from jax.experimental import pallas as pl
from jax.experimental.pallas import tpu as pltpu

F32 = jnp.float32
BF16 = jnp.bfloat16

NORM_EPS = 1e-6
LANES = 128
SUBLANES = 8
VMEM_LIMIT = 56 * 1024 * 1024

S5_GROUP_DIM = 16
S5_STATE = 64
SSD_HEAD_DIM = 64
SSD_GROUPS = 4
SSD_STATE = 128
SSD_CONV = 4
SSD_CHUNK = 128
XA_HEADS = 4
FFN_CONV = 3


def _cparams(n_axes, sequential):
    sem = ("arbitrary" if sequential else "parallel",) * n_axes
    return pltpu.CompilerParams(dimension_semantics=sem, vmem_limit_bytes=VMEM_LIMIT)


def _const_spec(shape):
    nd = len(shape)
    return pl.BlockSpec(shape, lambda *_: (0,) * nd)


def _rms(xf, w):
    ms = jnp.mean(xf * xf, axis=-1, keepdims=True)
    return xf * lax.rsqrt(ms + NORM_EPS) * w


def _dot(a, b):
    return jnp.dot(a, b, preferred_element_type=F32)


def _split3(x):
    hi = x.astype(BF16)
    r1 = x - hi.astype(F32)
    mid = r1.astype(BF16)
    lo = (r1 - mid.astype(F32)).astype(BF16)
    return hi, mid, lo


def _proj_kernel(x_ref, nw_ref, w_ref, u_ref, z_ref, xbc_ref, dt_ref, *, c_u, c_z, c_xbc):
    h = _rms(x_ref[...], nw_ref[...]).astype(BF16)
    u = _dot(h, w_ref[:, 0:c_u])
    for w in range(c_u // LANES):
        u_ref[w] = u[:, w * LANES:(w + 1) * LANES]
    z_ref[...] = _dot(h, w_ref[:, c_u:c_z])
    xbc_ref[...] = _dot(h, w_ref[:, c_z:c_xbc])
    dt_ref[...] = _dot(h, w_ref[:, c_xbc:c_xbc + LANES])


def _in_proj(x2, norm_w, w_in_p, c_u, c_z, c_xbc, tm):
    t, d = x2.shape
    kern = functools.partial(_proj_kernel, c_u=c_u, c_z=c_z, c_xbc=c_xbc)
    row = lambda w: pl.BlockSpec((tm, w), lambda i: (i, 0))
    return pl.pallas_call(
        kern,
        grid=(t // tm,),
        in_specs=[row(d), _const_spec(norm_w.shape), _const_spec(w_in_p.shape)],
        out_specs=[pl.BlockSpec((c_u // LANES, tm, LANES), lambda i: (0, i, 0)),
                   row(c_z - c_u), row(c_xbc - c_z), row(LANES)],
        out_shape=[jax.ShapeDtypeStruct((c_u // LANES, t, LANES), F32),
                   jax.ShapeDtypeStruct((t, c_z - c_u), F32),
                   jax.ShapeDtypeStruct((t, c_xbc - c_z), F32),
                   jax.ShapeDtypeStruct((t, LANES), F32)],
        compiler_params=_cparams(1, False),
        name="in_proj",
    )(x2, norm_w, w_in_p)


def _s5_kernel(u_ref, bblk_ref, lampow_ref, cblk_ref, d_ref, wglu_ref, y_ref,
               bu_ref, h_ref, yp_ref, carry_ref, cin_ref, *, seg_len, n_state):
    r_len = seg_len
    ns = n_state

    @pl.when(pl.program_id(1) == 0)
    def _():
        carry_ref[...] = jnp.zeros_like(carry_ref)

    nt = ns // LANES
    nw = u_ref.shape[0]

    u_perm = jnp.concatenate(
        [jnp.concatenate([u_ref[w, pl.ds(r, SUBLANES, stride=r_len), :] for w in range(nw)],
                         axis=1) for r in range(r_len)], axis=0)
    bu_ref[...] = _dot(u_perm.astype(BF16), bblk_ref[...])

    def lanes(ref, r0, r1, j):
        return ref[r0:r1, j * LANES:(j + 1) * LANES]

    a_re = [jnp.broadcast_to(lanes(lampow_ref, 0, 1, j), (SUBLANES, LANES)) for j in range(nt)]
    a_im = [jnp.broadcast_to(lanes(lampow_ref, 0, 1, nt + j), (SUBLANES, LANES))
            for j in range(nt)]

    h_re = [jnp.zeros((SUBLANES, LANES), F32) for _ in range(nt)]
    h_im = [jnp.zeros((SUBLANES, LANES), F32) for _ in range(nt)]
    for r in range(r_len):
        r0, r1 = r * SUBLANES, (r + 1) * SUBLANES
        for j in range(nt):
            n_re = a_re[j] * h_re[j] - a_im[j] * h_im[j] + lanes(bu_ref, r0, r1, j)
            n_im = a_re[j] * h_im[j] + a_im[j] * h_re[j] + lanes(bu_ref, r0, r1, nt + j)
            h_re[j], h_im[j] = n_re, n_im
            h_ref[r0:r1, j * LANES:(j + 1) * LANES] = n_re
            h_ref[r0:r1, (nt + j) * LANES:(nt + j + 1) * LANES] = n_im

    for j in range(nt):
        q_re = lanes(lampow_ref, r_len - 1, r_len, j)
        q_im = lanes(lampow_ref, r_len - 1, r_len, nt + j)
        c_re = lanes(carry_ref, 0, 1, j)
        c_im = lanes(carry_ref, 0, 1, nt + j)
        for s in range(SUBLANES):
            cin_ref[s:s + 1, j * LANES:(j + 1) * LANES] = c_re
            cin_ref[s:s + 1, (nt + j) * LANES:(nt + j + 1) * LANES] = c_im
            e_re = h_re[j][s:s + 1, :]
            e_im = h_im[j][s:s + 1, :]
            c_re, c_im = (e_re + q_re * c_re - q_im * c_im,
                          e_im + q_re * c_im + q_im * c_re)
        carry_ref[0:1, j * LANES:(j + 1) * LANES] = c_re
        carry_ref[0:1, (nt + j) * LANES:(nt + j + 1) * LANES] = c_im

    ci_re = [lanes(cin_ref, 0, SUBLANES, j) for j in range(nt)]
    ci_im = [lanes(cin_ref, 0, SUBLANES, nt + j) for j in range(nt)]
    for r in range(r_len):
        r0, r1 = r * SUBLANES, (r + 1) * SUBLANES
        for j in range(nt):
            p_re = lanes(lampow_ref, r, r + 1, j)
            p_im = lanes(lampow_ref, r, r + 1, nt + j)
            h_ref[r0:r1, j * LANES:(j + 1) * LANES] = (
                lanes(h_ref, r0, r1, j) + (p_re * ci_re[j] - p_im * ci_im[j]))
            h_ref[r0:r1, (nt + j) * LANES:(nt + j + 1) * LANES] = (
                lanes(h_ref, r0, r1, nt + j) + (p_re * ci_im[j] + p_im * ci_re[j]))

    yp = _dot(h_ref[...].astype(BF16), cblk_ref[...])
    for r in range(r_len):
        for w in range(nw):
            yp_ref[w, pl.ds(r, SUBLANES, stride=r_len), :] = (
                yp[r * SUBLANES:(r + 1) * SUBLANES, w * LANES:(w + 1) * LANES])
    u = jnp.concatenate([u_ref[w] for w in range(nw)], axis=1)
    y = jnp.concatenate([yp_ref[w] for w in range(nw)], axis=1) + u * d_ref[...]
    y = jax.nn.gelu(y)
    gate = jax.nn.sigmoid(_dot(y.astype(BF16), wglu_ref[...]))
    y_ref[...] = (y * gate).astype(y_ref.dtype)


def _s5_mixer(u4, bblk, lampow, cblk, d_skip, w_glu, tm):
    nw, b, l, _ = u4.shape
    w = nw * LANES
    ns = bblk.shape[1] // 2
    seg_len = tm // SUBLANES
    kern = functools.partial(_s5_kernel, seg_len=seg_len, n_state=ns)
    return pl.pallas_call(
        kern,
        grid=(b, l // tm),
        in_specs=[pl.BlockSpec((nw, None, tm, LANES), lambda bi, i: (0, bi, i, 0)),
                  _const_spec(bblk.shape), _const_spec(lampow.shape),
                  _const_spec(cblk.shape), _const_spec(d_skip.shape),
                  _const_spec(w_glu.shape)],
        out_specs=pl.BlockSpec((None, tm, w), lambda bi, i: (bi, i, 0)),
        out_shape=jax.ShapeDtypeStruct((b, l, w), BF16),
        scratch_shapes=[pltpu.VMEM((tm, 2 * ns), F32),
                        pltpu.VMEM((tm, 2 * ns), F32),
                        pltpu.VMEM((nw, tm, LANES), F32),
                        pltpu.VMEM((SUBLANES, 2 * ns), F32),
                        pltpu.VMEM((SUBLANES, 2 * ns), F32)],
        compiler_params=_cparams(2, True),
        name="s5_mixer",
    )(u4, bblk, lampow, cblk, d_skip, w_glu)


def _ssd_kernel(z_ref, xbc_ref, dt_ref, cw_ref, cb_ref, dtb_ref, alog_ref, dsk_ref,
                nw_ref, tri_ref, y_ref, cat_ref, state_ref, ybuf_ref,
                *, n_heads, inner):
    tm = z_ref.shape[0]
    q = SSD_CHUNK
    hd = SSD_HEAD_DIM
    hpg = n_heads // SSD_GROUPS
    halo = SUBLANES
    gs = SSD_GROUPS * SSD_STATE

    @pl.when(pl.program_id(1) == 0)
    def _():
        cat_ref[0:halo, :] = jnp.zeros((halo, cat_ref.shape[1]), F32)
        state_ref[...] = jnp.zeros_like(state_ref)

    cat_ref[halo:halo + tm, :] = xbc_ref[...]
    acc = cb_ref[...] + cw_ref[SSD_CONV - 1:SSD_CONV, :] * cat_ref[halo:halo + tm, :]
    for k in range(1, SSD_CONV):
        acc = acc + (cw_ref[SSD_CONV - 1 - k:SSD_CONV - k, :]
                     * cat_ref[pl.ds(halo - k, tm), :])
    cat_ref[0:halo, :] = cat_ref[tm:tm + halo, :]
    xbc = acc * jax.nn.sigmoid(acc)

    dtx = dt_ref[...] + dtb_ref[...]
    dtv = jnp.maximum(dtx, 0.0) + jnp.log(1.0 + jnp.exp(-jnp.abs(dtx)))
    a_neg = -jnp.exp(alog_ref[...])
    da = dtv * a_neg
    tri = tri_ref[...]
    li = lax.broadcasted_iota(jnp.int32, (q, q), 0)
    si = lax.broadcasted_iota(jnp.int32, (q, q), 1)
    causal = li >= si

    hpt = LANES // hd
    n_tiles = n_heads // hpt
    win = [(g * hpg * hd) // LANES for g in range(SSD_GROUPS)]
    assert all((g + 1) * hpg * hd <= (win[g] + 2) * LANES for g in range(SSD_GROUPS))
    first_head = lax.broadcasted_iota(jnp.int32, (q, LANES), 1) < hd
    first_head_row = lax.broadcasted_iota(jnp.int32, (1, LANES), 1) < hd
    win_head = lax.broadcasted_iota(jnp.int32, (q, 2 * LANES), 1) // hd

    def expand(src, rows, sel):
        return [jnp.where(sel,
                          jnp.broadcast_to(src[:, hpt * v:hpt * v + 1], (rows, LANES)),
                          jnp.broadcast_to(src[:, hpt * v + 1:hpt * v + 2], (rows, LANES)))
                for v in range(n_tiles)]

    def window(tiles, g):
        return jnp.concatenate([tiles[win[g]], tiles[win[g] + 1]], axis=1)

    for c in range(tm // q):
        r0 = c * q
        hi, mid, lo = _split3(da[r0:r0 + q, :])
        a_cs = _dot(tri, hi) + _dot(tri, mid) + _dot(tri, lo)
        a_cs_t = a_cs.T
        dt_c = dtv[r0:r0 + q, :]
        dt_t = dt_c.T
        a_last = a_cs[q - 1:q, :]
        e_t = expand(jnp.exp(a_cs), q, first_head)
        w_t = expand(jnp.exp(a_last - a_cs) * dt_c, q, first_head)
        d_t = expand(jnp.exp(a_last), 1, first_head_row)
        y_win = []
        for g in range(SSD_GROUPS):
            lanes_w = slice(win[g] * LANES, (win[g] + 2) * LANES)
            x_w = xbc[r0:r0 + q, lanes_w]
            x_bf = x_w.astype(BF16)
            b_g = xbc[r0:r0 + q, inner + g * SSD_STATE:inner + (g + 1) * SSD_STATE]
            c_g = xbc[r0:r0 + q, inner + gs + g * SSD_STATE:inner + gs + (g + 1) * SSD_STATE]
            b_bf = b_g.astype(BF16)
            c_bf = c_g.astype(BF16)
            cb = lax.dot_general(c_bf, b_bf, (((1,), (1,)), ((), ())),
                                 preferred_element_type=F32)
            y_diag = None
            for j in range(hpg):
                h = g * hpg + j
                col = a_cs[:, h:h + 1]
                row = a_cs_t[h:h + 1, :]
                lmat = jnp.exp(jnp.where(causal, col - row, -jnp.inf))
                m = (cb * lmat * dt_t[h:h + 1, :]).astype(BF16)
                r_j = _dot(m, x_bf)
                y_diag = r_j if y_diag is None else jnp.where(
                    win_head == h - win[g] * hpt, r_j, y_diag)
            s_prev = state_ref[g]
            y_off = _dot(c_bf, s_prev.astype(BF16)) * window(e_t, g)
            xw = (x_w * window(w_t, g)).astype(BF16)
            upd = lax.dot_general(b_bf, xw, (((0,), (0,)), ((), ())),
                                  preferred_element_type=F32)
            state_ref[g] = s_prev * window(d_t, g) + upd
            y_win.append(y_diag + y_off + x_w * dsk_ref[0:1, lanes_w])
        tiles = []
        for v in range(n_tiles):
            ga, gb = (hpt * v) // hpg, (hpt * v + 1) // hpg
            ta = y_win[ga][:, (v - win[ga]) * LANES:(v - win[ga] + 1) * LANES]
            if ga == gb:
                tiles.append(ta)
            else:
                tb = y_win[gb][:, (v - win[gb]) * LANES:(v - win[gb] + 1) * LANES]
                tiles.append(jnp.where(first_head, ta, tb))
        ybuf_ref[r0:r0 + q, :] = jnp.concatenate(tiles, axis=1)

    zz = z_ref[...]
    gated = ybuf_ref[...] * (zz * jax.nn.sigmoid(zz))
    y_ref[...] = _rms(gated, nw_ref[...]).astype(y_ref.dtype)


def _ssd_mixer(z3, xbc3, dt3, conv_w, conv_b, dt_bias, a_log, d_skip, norm_w, tri, n_heads, tm):
    b, l, inner = z3.shape
    cdim = xbc3.shape[2]
    kern = functools.partial(_ssd_kernel, n_heads=n_heads, inner=inner)
    blk = lambda w: pl.BlockSpec((None, tm, w), lambda bi, i: (bi, i, 0))
    return pl.pallas_call(
        kern,
        grid=(b, l // tm),
        in_specs=[blk(inner), blk(cdim), blk(LANES),
                  _const_spec(conv_w.shape), _const_spec(conv_b.shape),
                  _const_spec(dt_bias.shape), _const_spec(a_log.shape),
                  _const_spec(d_skip.shape), _const_spec(norm_w.shape),
                  _const_spec(tri.shape)],
        out_specs=blk(inner),
        out_shape=jax.ShapeDtypeStruct((b, l, inner), BF16),
        scratch_shapes=[pltpu.VMEM((tm + SUBLANES, cdim), F32),
                        pltpu.VMEM((SSD_GROUPS, SSD_STATE, 2 * LANES), F32),
                        pltpu.VMEM((tm, inner), F32)],
        compiler_params=_cparams(2, True),
        name="ssd_mixer",
    )(z3, xbc3, dt3, conv_w, conv_b, dt_bias, a_log, d_skip, norm_w, tri)


def _kv_kernel(mem_ref, nw_ref, wk_ref, wv_ref, wq_ref, wo_ref, ws_ref, wvo_ref):
    m = _rms(mem_ref[...], nw_ref[...]).astype(BF16)
    n_mem, d = mem_ref.shape
    k = _dot(m, wk_ref[...]).astype(BF16)
    v = _dot(m, wv_ref[...]).astype(BF16)
    hd = d // XA_HEADS
    scale = hd ** -0.5
    for hh in range(XA_HEADS):
        cols = slice(hh * hd, (hh + 1) * hd)
        mcols = slice(hh * n_mem, (hh + 1) * n_mem)
        qk = lax.dot_general(wq_ref[:, cols], k[:, cols], (((1,), (1,)), ((), ())),
                             preferred_element_type=F32)
        ws_ref[:, mcols] = (qk * scale).astype(ws_ref.dtype)
        wvo_ref[mcols, :] = _dot(v[:, cols], wo_ref[cols, :]).astype(wvo_ref.dtype)


def _mem_kv(mem, mem_norm_w, wk, wv, wq, wo):
    b, m, d = mem.shape
    nl = wk.shape[0]
    hm = XA_HEADS * m
    wspec = pl.BlockSpec((None, d, d), lambda li, bi: (li, 0, 0))
    return pl.pallas_call(
        _kv_kernel,
        grid=(nl, b),
        in_specs=[pl.BlockSpec((None, m, d), lambda li, bi: (bi, 0, 0)),
                  pl.BlockSpec((None, 1, d), lambda li, bi: (li, 0, 0)),
                  wspec, wspec, wspec, wspec],
        out_specs=[pl.BlockSpec((None, None, d, hm), lambda li, bi: (li, bi, 0, 0)),
                   pl.BlockSpec((None, None, hm, d), lambda li, bi: (li, bi, 0, 0))],
        out_shape=[jax.ShapeDtypeStruct((nl, b, d, hm), BF16),
                   jax.ShapeDtypeStruct((nl, b, hm, d), BF16)],
        compiler_params=_cparams(2, False),
        name="mem_kv",
    )(mem, mem_norm_w, wk, wv, wq, wo)


def _xattn_kernel(x_ref, ys5_ref, yssd_ref, wo1_ref, wo2_ref, nw_ref, ws_ref, wvo_ref,
                  o_ref, p_ref):
    x1 = x_ref[...] + _dot(ys5_ref[...], wo1_ref[...]) + _dot(yssd_ref[...], wo2_ref[...])
    h = _rms(x1, nw_ref[...]).astype(BF16)
    s_all = _dot(h, ws_ref[...])
    n_mem = ws_ref.shape[1] // XA_HEADS
    for hh in range(XA_HEADS):
        mcols = slice(hh * n_mem, (hh + 1) * n_mem)
        s = s_all[:, mcols]
        s = s - jnp.max(s, axis=-1, keepdims=True)
        p = jnp.exp(s)
        p = p / jnp.sum(p, axis=-1, keepdims=True)
        p_ref[:, mcols] = p.astype(p_ref.dtype)
    o_ref[...] = x1 + _dot(p_ref[...], wvo_ref[...])


def _xattn(x3, ys5, yssd, wo1, wo2, norm_w, ws, wvo, tm):
    b, l, d = x3.shape
    hm = ws.shape[2]
    blk = lambda w: pl.BlockSpec((None, tm, w), lambda bi, i: (bi, i, 0))
    return pl.pallas_call(
        _xattn_kernel,
        grid=(b, l // tm),
        in_specs=[blk(d), blk(ys5.shape[2]), blk(yssd.shape[2]),
                  _const_spec(wo1.shape), _const_spec(wo2.shape), _const_spec(norm_w.shape),
                  pl.BlockSpec((None, d, hm), lambda bi, i: (bi, 0, 0)),
                  pl.BlockSpec((None, hm, d), lambda bi, i: (bi, 0, 0))],
        out_specs=blk(d),
        out_shape=jax.ShapeDtypeStruct((b, l, d), F32),
        scratch_shapes=[pltpu.VMEM((tm, hm), BF16)],
        compiler_params=_cparams(2, False),
        name="xattn",
    )(x3, ys5, yssd, wo1, wo2, norm_w, ws, wvo)


def _ffn_kernel(x_ref, nw_ref, wup_ref, cw_ref, cb_ref, wdn_ref, fnw_ref, o_ref,
                halo_ref, act_ref, *, d_ff, chunk, final_norm):
    tm = x_ref.shape[0]

    @pl.when(pl.program_id(1) == 0)
    def _():
        halo_ref[...] = jnp.zeros_like(halo_ref)

    x = x_ref[...]
    h = _rms(x, nw_ref[...]).astype(BF16)
    row = lax.broadcasted_iota(jnp.int32, (tm, chunk), 0)
    for c in range(d_ff // chunk):
        cols = slice(c * chunk, (c + 1) * chunk)
        g = _dot(h, wup_ref[:, cols])
        v = _dot(h, wup_ref[:, d_ff + c * chunk:d_ff + (c + 1) * chunk])
        p1 = halo_ref[SUBLANES - 1:SUBLANES, cols]
        p2 = halo_ref[SUBLANES - 2:SUBLANES - 1, cols]
        gm1 = jnp.where(row == 0, p1, pltpu.roll(g, 1, 0))
        gm2 = jnp.where(row == 0, p2, jnp.where(row == 1, p1, pltpu.roll(g, 2, 0)))
        halo_ref[:, cols] = g[tm - SUBLANES:tm, :]
        y = (cb_ref[0:1, cols] + cw_ref[0:1, cols] * gm2 + cw_ref[1:2, cols] * gm1
             + cw_ref[2:3, cols] * g)
        act_ref[:, cols] = (y * jax.nn.sigmoid(y) * v).astype(BF16)
    out = x + _dot(act_ref[...], wdn_ref[...])
    if final_norm:
        out = _rms(out, fnw_ref[...])
    o_ref[...] = out


def _ffn(x3, norm_w, w_up, conv_w, conv_b, w_down, final_w, final_norm, tm, chunk):
    b, l, d = x3.shape
    d_ff = w_down.shape[0]
    kern = functools.partial(_ffn_kernel, d_ff=d_ff, chunk=chunk, final_norm=final_norm)
    blk = pl.BlockSpec((None, tm, d), lambda bi, i: (bi, i, 0))
    return pl.pallas_call(
        kern,
        grid=(b, l // tm),
        in_specs=[blk, _const_spec(norm_w.shape), _const_spec(w_up.shape),
                  _const_spec(conv_w.shape), _const_spec(conv_b.shape),
                  _const_spec(w_down.shape), _const_spec(final_w.shape)],
        out_specs=blk,
        out_shape=jax.ShapeDtypeStruct((b, l, d), F32),
        scratch_shapes=[pltpu.VMEM((SUBLANES, d_ff), F32),
                        pltpu.VMEM((tm, d_ff), BF16)],
        compiler_params=_cparams(2, True),
        name="conv_ffn",
    )(x3, norm_w, w_up, conv_w, conv_b, w_down, final_w)


def _s5_params(lam_re, lam_im, log_dt, b_re, b_im, c_re, c_im, seg_len):
    g, p = lam_re.shape
    cg = b_re.shape[2]
    lr = jnp.minimum(lam_re.astype(F32), -1e-4)
    li = lam_im.astype(F32)
    dt = jnp.exp(log_dt.astype(F32))[:, None]
    mag = jnp.exp(lr * dt)
    ab_re = mag * jnp.cos(li * dt)
    ab_im = mag * jnp.sin(li * dt)
    den = lr * lr + li * li
    f_re = ((ab_re - 1.0) * lr + ab_im * li) / den
    f_im = (ab_im * lr - (ab_re - 1.0) * li) / den
    br = b_re.astype(F32)
    bi = b_im.astype(F32)
    bb_re = f_re[..., None] * br - f_im[..., None] * bi
    bb_im = f_re[..., None] * bi + f_im[..., None] * br
    eye = jnp.eye(g, dtype=F32)

    def in_blk(bb):
        return jnp.einsum('gpc,gh->gchp', bb, eye).reshape(g * cg, g * p)

    def out_blk(cc):
        return jnp.einsum('gcp,gh->gphc', cc, eye).reshape(g * p, g * cg)

    bblk = jnp.concatenate([in_blk(bb_re), in_blk(bb_im)], axis=1).astype(BF16)
    cblk = jnp.concatenate([out_blk(c_re.astype(F32)), -out_blk(c_im.astype(F32))],
                           axis=0).astype(BF16)
    steps = jnp.arange(1, seg_len + 1, dtype=F32)[:, None]
    lam_log = (lr * dt).reshape(1, g * p)
    lam_ang = (li * dt).reshape(1, g * p)
    pmag = jnp.exp(steps * lam_log)
    lampow = jnp.concatenate([pmag * jnp.cos(steps * lam_ang),
                              pmag * jnp.sin(steps * lam_ang)], axis=1)
    lampow = lampow.at[0].set(jnp.concatenate([ab_re.reshape(-1), ab_im.reshape(-1)]))
    return bblk, lampow, cblk


def _pad_lanes(v, width):
    v = v.reshape(1, -1).astype(F32)
    return jnp.pad(v, ((0, 0), (0, width - v.shape[1])))


TM_PROJ = 512
TM_S5 = 512
TM_SSD = 512
TM_XA = 512
TM_FFN = 512
FFN_CHUNK = 256


def kernel(x, mem, mix_norm_w, w_in, s5_lambda_re, s5_lambda_im, s5_log_dt, s5_b_re, s5_b_im,
           s5_c_re, s5_c_im, s5_d, s5_w_glu, ssd_conv_w, ssd_conv_b, ssd_dt_bias, ssd_a_log,
           ssd_d, ssd_norm_w, w_out, xa_norm_w, mem_norm_w, xa_wq, xa_wk, xa_wv, xa_wo,
           ffn_norm_w, ffn_w_up, ffn_conv_w, ffn_conv_b, ffn_w_down, final_norm_w):
    bsz, seq, d = x.shape
    depth = w_in.shape[0]
    s5_w = s5_d.shape[1]
    inner = ssd_norm_w.shape[1]
    n_heads = ssd_a_log.shape[1]
    cdim = ssd_conv_w.shape[2]
    c_u, c_z, c_xbc = s5_w, s5_w + inner, s5_w + inner + cdim
    in_cols = w_in.shape[2]
    t = bsz * seq

    tri = jnp.tril(jnp.ones((SSD_CHUNK, SSD_CHUNK), F32)).astype(BF16)
    ws_all, wvo_all = _mem_kv(mem, mem_norm_w.reshape(depth, 1, d), xa_wk.astype(BF16),
                              xa_wv.astype(BF16), xa_wq.astype(BF16), xa_wo.astype(BF16))

    for i in range(depth):
        w_in_p = jnp.pad(w_in[i], ((0, 0), (0, c_xbc + LANES - in_cols))).astype(BF16)
        u, z, xbc, dtr = _in_proj(x.reshape(t, d), mix_norm_w[i].reshape(1, d), w_in_p,
                                  c_u, c_z, c_xbc, TM_PROJ)
        bblk, lampow, cblk = _s5_params(s5_lambda_re[i], s5_lambda_im[i], s5_log_dt[i],
                                        s5_b_re[i], s5_b_im[i], s5_c_re[i], s5_c_im[i],
                                        TM_S5 // SUBLANES)
        y_s5 = _s5_mixer(u.reshape(c_u // LANES, bsz, seq, LANES), bblk, lampow, cblk,
                         s5_d[i].reshape(1, s5_w).astype(F32), s5_w_glu[i].astype(BF16), TM_S5)
        y_ssd = _ssd_mixer(z.reshape(bsz, seq, inner), xbc.reshape(bsz, seq, cdim),
                           dtr.reshape(bsz, seq, LANES),
                           ssd_conv_w[i].astype(F32), ssd_conv_b[i].reshape(1, cdim).astype(F32),
                           _pad_lanes(ssd_dt_bias[i], LANES), _pad_lanes(ssd_a_log[i], LANES),
                           jnp.repeat(ssd_d[i].astype(F32), SSD_HEAD_DIM).reshape(1, inner),
                           ssd_norm_w[i].reshape(1, inner).astype(F32),
                           tri, n_heads, TM_SSD)
        wo_bf = w_out[i].astype(BF16)
        x = _xattn(x, y_s5, y_ssd, wo_bf[:s5_w], wo_bf[s5_w:], xa_norm_w[i].reshape(1, d),
                   ws_all[i], wvo_all[i], TM_XA)
        x = _ffn(x, ffn_norm_w[i].reshape(1, d), ffn_w_up[i].astype(BF16),
                 ffn_conv_w[i].astype(F32), ffn_conv_b[i].reshape(1, -1).astype(F32),
                 ffn_w_down[i].astype(BF16), final_norm_w.reshape(1, d),
                 i == depth - 1, TM_FFN, FFN_CHUNK)
    return x
```

```python
import functools

import jax
import jax.numpy as jnp
from jax import lax
from jax.experimental import pallas as pl
from jax.experimental.pallas import tpu as pltpu

F32 = jnp.float32
BF16 = jnp.bfloat16

NORM_EPS = 1e-6
LANES = 128
SUBLANES = 8
VMEM_LIMIT = 56 * 1024 * 1024

SSD_HEAD_DIM = 64
SSD_GROUPS = 4
SSD_STATE = 128
SSD_CONV = 4
SSD_CHUNK = 128
XA_HEADS = 4

TM_MIX = 512
TM_XA = 512
TM_FFN = 512
FFN_CHUNK = 256


def _cparams(n_axes, sequential):
    sem = ("arbitrary" if sequential else "parallel",) * n_axes
    return pltpu.CompilerParams(dimension_semantics=sem, vmem_limit_bytes=VMEM_LIMIT)


def _layer_spec(stacked, layer):
    nd = stacked.ndim - 1
    return pl.BlockSpec((None,) + stacked.shape[1:], lambda *_: (layer,) + (0,) * nd,
                        pipeline_mode=pl.Buffered(1))


def _rms(xf, w):
    ms = jnp.mean(xf * xf, axis=-1, keepdims=True)
    return xf * lax.rsqrt(ms + NORM_EPS) * w


def _dot(a, b):
    return jnp.dot(a, b, preferred_element_type=F32)


def _split3(x):
    hi = x.astype(BF16)
    r1 = x - hi.astype(F32)
    mid = r1.astype(BF16)
    lo = (r1 - mid.astype(F32)).astype(BF16)
    return hi, mid, lo


def _s5_gather(u_ref):
    nw, tm, _ = u_ref.shape
    r_len = tm // SUBLANES
    u_perm = jnp.concatenate(
        [jnp.concatenate([u_ref[w, pl.ds(r, SUBLANES, stride=r_len), :] for w in range(nw)],
                         axis=1) for r in range(r_len)], axis=0)
    return u_perm, jnp.concatenate([u_ref[w] for w in range(nw)], axis=1)


def _s5_body(u_perm, u, bblk_ref, lampow_ref, cblk_ref, d_ref, wglu_ref,
             st_ref, yp_ref, carry_ref, cin_ref, pump, pump_every):
    tm, width = u.shape
    nw = width // LANES
    r_len = tm // SUBLANES
    ns = bblk_ref.shape[1] // 2
    nt = ns // LANES

    st_ref[...] = _dot(u_perm.astype(BF16), bblk_ref[...])

    def lanes(ref, r0, r1, j):
        return ref[r0:r1, j * LANES:(j + 1) * LANES]

    a_re = [jnp.broadcast_to(lanes(lampow_ref, 0, 1, j), (SUBLANES, LANES)) for j in range(nt)]
    a_im = [jnp.broadcast_to(lanes(lampow_ref, 0, 1, nt + j), (SUBLANES, LANES))
            for j in range(nt)]

    def scan(h_re, h_im, keep):
        for r in range(r_len):
            r0, r1 = r * SUBLANES, (r + 1) * SUBLANES
            for j in range(nt):
                n_re = a_re[j] * h_re[j] - a_im[j] * h_im[j] + lanes(st_ref, r0, r1, j)
                n_im = a_re[j] * h_im[j] + a_im[j] * h_re[j] + lanes(st_ref, r0, r1, nt + j)
                h_re[j], h_im[j] = n_re, n_im
                if keep:
                    st_ref[r0:r1, j * LANES:(j + 1) * LANES] = n_re
                    st_ref[r0:r1, (nt + j) * LANES:(nt + j + 1) * LANES] = n_im
            if r % pump_every == pump_every - 1:
                pump()
        return h_re, h_im

    h_re, h_im = scan([jnp.zeros((SUBLANES, LANES), F32) for _ in range(nt)],
                      [jnp.zeros((SUBLANES, LANES), F32) for _ in range(nt)], keep=False)

    for j in range(nt):
        q_re = lanes(lampow_ref, 1, 2, j)
        q_im = lanes(lampow_ref, 1, 2, nt + j)
        c_re = lanes(carry_ref, 0, 1, j)
        c_im = lanes(carry_ref, 0, 1, nt + j)
        for s in range(SUBLANES):
            cin_ref[s:s + 1, j * LANES:(j + 1) * LANES] = c_re
            cin_ref[s:s + 1, (nt + j) * LANES:(nt + j + 1) * LANES] = c_im
            e_re = h_re[j][s:s + 1, :]
            e_im = h_im[j][s:s + 1, :]
            c_re, c_im = (e_re + q_re * c_re - q_im * c_im,
                          e_im + q_re * c_im + q_im * c_re)
        carry_ref[0:1, j * LANES:(j + 1) * LANES] = c_re
        carry_ref[0:1, (nt + j) * LANES:(nt + j + 1) * LANES] = c_im

    scan([lanes(cin_ref, 0, SUBLANES, j) for j in range(nt)],
         [lanes(cin_ref, 0, SUBLANES, nt + j) for j in range(nt)], keep=True)

    yp = _dot(st_ref[...].astype(BF16), cblk_ref[...])
    for r in range(r_len):
        for w in range(nw):
            yp_ref[w, pl.ds(r, SUBLANES, stride=r_len), :] = (
                yp[r * SUBLANES:(r + 1) * SUBLANES, w * LANES:(w + 1) * LANES])
    y = jnp.concatenate([yp_ref[w] for w in range(nw)], axis=1) + u * d_ref[...]
    y = jax.nn.gelu(y)
    gate = jax.nn.sigmoid(_dot(y.astype(BF16), wglu_ref[...]))
    return y * gate


def _ssd_conv(cat_ref, cw_ref, cb_ref):
    halo = SUBLANES
    tm = cat_ref.shape[0] - halo
    acc = cb_ref[...] + cw_ref[SSD_CONV - 1:SSD_CONV, :] * cat_ref[halo:halo + tm, :]
    for k in range(1, SSD_CONV):
        acc = acc + (cw_ref[SSD_CONV - 1 - k:SSD_CONV - k, :]
                     * cat_ref[pl.ds(halo - k, tm), :])
    cat_ref[0:halo, :] = cat_ref[tm:tm + halo, :]
    return acc * jax.nn.sigmoid(acc)


def _ssd_body(xbc, z_gate, dtv, alog_ref, dsk_ref, nw_ref, tri_ref, state_ref, ybuf_ref,
              pump, *, n_heads, inner):
    tm = xbc.shape[0]
    q = SSD_CHUNK
    hd = SSD_HEAD_DIM
    hpg = n_heads // SSD_GROUPS
    gs = SSD_GROUPS * SSD_STATE

    a_neg = -jnp.exp(alog_ref[...])
    da = dtv * a_neg
    tri = tri_ref[...]
    li = lax.broadcasted_iota(jnp.int32, (q, q), 0)
    si = lax.broadcasted_iota(jnp.int32, (q, q), 1)
    causal = li >= si

    hpt = LANES // hd
    n_tiles = n_heads // hpt
    win = [(g * hpg * hd) // LANES for g in range(SSD_GROUPS)]
    assert all((g + 1) * hpg * hd <= (win[g] + 2) * LANES for g in range(SSD_GROUPS))
    first_head = lax.broadcasted_iota(jnp.int32, (q, LANES), 1) < hd
    first_head_row = lax.broadcasted_iota(jnp.int32, (1, LANES), 1) < hd
    win_head = lax.broadcasted_iota(jnp.int32, (q, 2 * LANES), 1) // hd

    def expand(src, rows, sel):
        return [jnp.where(sel,
                          jnp.broadcast_to(src[:, hpt * v:hpt * v + 1], (rows, LANES)),
                          jnp.broadcast_to(src[:, hpt * v + 1:hpt * v + 2], (rows, LANES)))
                for v in range(n_tiles)]

    def window(tiles, g):
        return jnp.concatenate([tiles[win[g]], tiles[win[g] + 1]], axis=1)

    for c in range(tm // q):
        r0 = c * q
        hi, mid, lo = _split3(da[r0:r0 + q, :])
        a_cs = _dot(tri, hi) + _dot(tri, mid) + _dot(tri, lo)
        a_cs_t = a_cs.T
        dt_c = dtv[r0:r0 + q, :]
        dt_t = dt_c.T
        a_last = a_cs[q - 1:q, :]
        e_t = expand(jnp.exp(a_cs), q, first_head)
        w_t = expand(jnp.exp(a_last - a_cs) * dt_c, q, first_head)
        d_t = expand(jnp.exp(a_last), 1, first_head_row)
        y_win = []
        for g in range(SSD_GROUPS):
            lanes_w = slice(win[g] * LANES, (win[g] + 2) * LANES)
            x_w = xbc[r0:r0 + q, lanes_w]
            x_bf = x_w.astype(BF16)
            b_g = xbc[r0:r0 + q, inner + g * SSD_STATE:inner + (g + 1) * SSD_STATE]
            c_g = xbc[r0:r0 + q, inner + gs + g * SSD_STATE:inner + gs + (g + 1) * SSD_STATE]
            b_bf = b_g.astype(BF16)
            c_bf = c_g.astype(BF16)
            cb = lax.dot_general(c_bf, b_bf, (((1,), (1,)), ((), ())),
                                 preferred_element_type=F32)
            y_diag = None
            for j in range(hpg):
                h = g * hpg + j
                col = a_cs[:, h:h + 1]
                row = a_cs_t[h:h + 1, :]
                lmat = jnp.exp(jnp.where(causal, col - row, -jnp.inf))
                m = (cb * lmat * dt_t[h:h + 1, :]).astype(BF16)
                r_j = _dot(m, x_bf)
                y_diag = r_j if y_diag is None else jnp.where(
                    win_head == h - win[g] * hpt, r_j, y_diag)
            s_prev = state_ref[g]
            y_off = _dot(c_bf, s_prev.astype(BF16)) * window(e_t, g)
            xw = (x_w * window(w_t, g)).astype(BF16)
            upd = lax.dot_general(b_bf, xw, (((0,), (0,)), ((), ())),
                                  preferred_element_type=F32)
            state_ref[g] = s_prev * window(d_t, g) + upd
            y_win.append(y_diag + y_off + x_w * dsk_ref[0:1, lanes_w])
            if g % 2 == 1:
                pump()
        tiles = []
        for v in range(n_tiles):
            ga, gb = (hpt * v) // hpg, (hpt * v + 1) // hpg
            ta = y_win[ga][:, (v - win[ga]) * LANES:(v - win[ga] + 1) * LANES]
            if ga == gb:
                tiles.append(ta)
            else:
                tb = y_win[gb][:, (v - win[gb]) * LANES:(v - win[gb] + 1) * LANES]
                tiles.append(jnp.where(first_head, ta, tb))
        ybuf_ref[r0:r0 + q, :] = jnp.concatenate(tiles, axis=1)

    return _rms(ybuf_ref[...] * z_gate, nw_ref[...])


def _proj_items(h, win_ref, u_ref, cat_ref, z_ref, dt_ref, c_u, c_z, c_xbc, width):
    halo = SUBLANES
    tm = h.shape[0]

    def u_job():
        u = _dot(h, win_ref[:, 0:c_u])
        for w in range(c_u // LANES):
            u_ref[w] = u[:, w * LANES:(w + 1) * LANES]

    def col_job(dst_ref, r0, c_src, c_dst, n):
        def job():
            dst_ref[r0:r0 + tm, c_dst:c_dst + n] = _dot(h, win_ref[:, c_src:c_src + n])
        return job

    jobs = [u_job]
    for c0 in range(0, c_xbc - c_z, width):
        jobs.append(col_job(cat_ref, halo, c_z + c0, c0, min(width, c_xbc - c_z - c0)))
    jobs.append(col_job(dt_ref, 0, c_xbc, 0, LANES))
    for c0 in range(0, c_z - c_u, width):
        jobs.append(col_job(z_ref, 0, c_u + c0, c0, min(width, c_z - c_u - c0)))
    return jobs


def _mixer_kernel(x_ref, xn_ref, nw_ref, win_ref,
                  bblk_ref, lampow_ref, cblk_ref, s5d_ref, wglu_ref,
                  cw_ref, cb_ref, dtb_ref, alog_ref, dsk_ref, snw_ref, tri_ref,
                  wo_ref, o_ref,
                  u_ref, z_ref, dt_ref, st_ref, yp_ref, carry_ref, cin_ref, cat_ref, state_ref,
                  ybuf_ref, *, c_u, c_z, c_xbc, n_heads):
    halo = SUBLANES
    proj = functools.partial(_proj_items, win_ref=win_ref, u_ref=u_ref, cat_ref=cat_ref,
                             z_ref=z_ref, dt_ref=dt_ref, c_u=c_u, c_z=c_z, c_xbc=c_xbc,
                             width=2 * LANES)

    @pl.when(pl.program_id(1) == 0)
    def _():
        carry_ref[...] = jnp.zeros_like(carry_ref)
        cat_ref[0:halo, :] = jnp.zeros((halo, cat_ref.shape[1]), F32)
        state_ref[...] = jnp.zeros_like(state_ref)
        for job in proj(_rms(x_ref[...], nw_ref[...]).astype(BF16)):
            job()

    u_perm, u = _s5_gather(u_ref)
    xbc = _ssd_conv(cat_ref, cw_ref, cb_ref)
    zz = z_ref[...]
    z_gate = zz * jax.nn.sigmoid(zz)
    dtx = dt_ref[...] + dtb_ref[...]
    dtv = jnp.maximum(dtx, 0.0) + jnp.log(1.0 + jnp.exp(-jnp.abs(dtx)))

    jobs = proj(_rms(xn_ref[...], nw_ref[...]).astype(BF16))

    def pump():
        if jobs:
            jobs.pop(0)()

    y_s5 = _s5_body(u_perm, u, bblk_ref, lampow_ref, cblk_ref, s5d_ref, wglu_ref,
                    st_ref, yp_ref, carry_ref, cin_ref, pump, pump_every=32)
    y_ssd = _ssd_body(xbc, z_gate, dtv, alog_ref, dsk_ref, snw_ref, tri_ref, state_ref, ybuf_ref,
                      pump, n_heads=n_heads, inner=c_z - c_u)
    while jobs:
        pump()
    o_ref[...] = (x_ref[...] + _dot(y_s5.astype(BF16), wo_ref[0:c_u, :])
                  + _dot(y_ssd.astype(BF16), wo_ref[c_u:c_z, :]))


def _mixer(x3, layer, norm_w, w_in_p, s5p, ssdp, w_out, c_u, c_z, c_xbc, n_heads, tm):
    b, l, d = x3.shape
    ns2 = s5p[0].shape[2]
    inner = c_z - c_u
    cdim = c_xbc - c_z
    n_t = l // tm
    consts = (norm_w, w_in_p) + tuple(s5p) + tuple(ssdp) + (w_out,)
    kern = functools.partial(_mixer_kernel, c_u=c_u, c_z=c_z, c_xbc=c_xbc, n_heads=n_heads)
    blk = pl.BlockSpec((None, tm, d), lambda bi, i: (bi, i, 0))
    nxt = pl.BlockSpec((None, tm, d), lambda bi, i: (bi, jnp.minimum(i + 1, n_t - 1), 0))
    return pl.pallas_call(
        kern,
        grid=(b, n_t),
        in_specs=[blk, nxt] + [_layer_spec(a, layer) for a in consts],
        out_specs=blk,
        out_shape=jax.ShapeDtypeStruct((b, l, d), F32),
        scratch_shapes=[pltpu.VMEM((c_u // LANES, tm, LANES), F32),
                        pltpu.VMEM((tm, inner), F32),
                        pltpu.VMEM((tm, LANES), F32),
                        pltpu.VMEM((tm, ns2), F32),
                        pltpu.VMEM((c_u // LANES, tm, LANES), F32),
                        pltpu.VMEM((SUBLANES, ns2), F32),
                        pltpu.VMEM((SUBLANES, ns2), F32),
                        pltpu.VMEM((tm + SUBLANES, cdim), F32),
                        pltpu.VMEM((SSD_GROUPS, SSD_STATE, 2 * LANES), F32),
                        pltpu.VMEM((tm, inner), F32)],
        compiler_params=_cparams(2, True),
        name="mixer",
    )(x3, x3, *consts)


def _kv_kernel(mem_ref, nw_ref, wk_ref, wv_ref, wq_ref, wo_ref, ws_ref, wvo_ref):
    m = _rms(mem_ref[...], nw_ref[...]).astype(BF16)
    n_mem, d = mem_ref.shape
    k = _dot(m, wk_ref[...]).astype(BF16)
    v = _dot(m, wv_ref[...]).astype(BF16)
    hd = d // XA_HEADS
    scale = hd ** -0.5
    for hh in range(XA_HEADS):
        cols = slice(hh * hd, (hh + 1) * hd)
        mcols = slice(hh * n_mem, (hh + 1) * n_mem)
        qk = lax.dot_general(wq_ref[:, cols], k[:, cols], (((1,), (1,)), ((), ())),
                             preferred_element_type=F32)
        ws_ref[:, mcols] = (qk * scale).astype(ws_ref.dtype)
        wvo_ref[mcols, :] = _dot(v[:, cols], wo_ref[cols, :]).astype(wvo_ref.dtype)


def _mem_kv(mem, mem_norm_w, wk, wv, wq, wo):
    b, m, d = mem.shape
    nl = wk.shape[0]
    hm = XA_HEADS * m
    wspec = pl.BlockSpec((None, d, d), lambda li, bi: (li, 0, 0))
    return pl.pallas_call(
        _kv_kernel,
        grid=(nl, b),
        in_specs=[pl.BlockSpec((None, m, d), lambda li, bi: (bi, 0, 0)),
                  pl.BlockSpec((None, 1, d), lambda li, bi: (li, 0, 0)),
                  wspec, wspec, wspec, wspec],
        out_specs=[pl.BlockSpec((None, None, d, hm), lambda li, bi: (li, bi, 0, 0)),
                   pl.BlockSpec((None, None, hm, d), lambda li, bi: (li, bi, 0, 0))],
        out_shape=[jax.ShapeDtypeStruct((nl, b, d, hm), BF16),
                   jax.ShapeDtypeStruct((nl, b, hm, d), BF16)],
        compiler_params=_cparams(2, False),
        name="mem_kv",
    )(mem, mem_norm_w, wk, wv, wq, wo)


def _xattn_kernel(x_ref, nw_ref, ws_ref, wvo_ref, o_ref, p_ref):
    x = x_ref[...]
    h = _rms(x, nw_ref[...]).astype(BF16)
    s_all = _dot(h, ws_ref[...])
    n_mem = ws_ref.shape[1] // XA_HEADS
    for hh in range(XA_HEADS):
        mcols = slice(hh * n_mem, (hh + 1) * n_mem)
        s = s_all[:, mcols]
        s = s - jnp.max(s, axis=-1, keepdims=True)
        p = jnp.exp(s)
        p = p / jnp.sum(p, axis=-1, keepdims=True)
        p_ref[:, mcols] = p.astype(p_ref.dtype)
    o_ref[...] = x + _dot(p_ref[...], wvo_ref[...])


def _xattn(x3, layer, norm_w, ws, wvo, tm):
    b, l, d = x3.shape
    hm = ws.shape[3]
    blk = pl.BlockSpec((None, tm, d), lambda bi, i: (bi, i, 0))
    return pl.pallas_call(
        _xattn_kernel,
        grid=(b, l // tm),
        in_specs=[blk, _layer_spec(norm_w, layer),
                  pl.BlockSpec((None, None, d, hm), lambda bi, i: (layer, bi, 0, 0)),
                  pl.BlockSpec((None, None, hm, d), lambda bi, i: (layer, bi, 0, 0))],
        out_specs=blk,
        out_shape=jax.ShapeDtypeStruct((b, l, d), F32),
        scratch_shapes=[pltpu.VMEM((tm, hm), BF16)],
        compiler_params=_cparams(2, False),
        name="xattn",
    )(x3, norm_w, ws, wvo)


def _ffn_kernel(x_ref, nw_ref, wup_ref, cw_ref, cb_ref, wdn_ref, fnw_ref, o_ref,
                halo_ref, act_ref, *, d_ff, chunk, final_norm):
    tm = x_ref.shape[0]

    @pl.when(pl.program_id(1) == 0)
    def _():
        halo_ref[...] = jnp.zeros_like(halo_ref)

    x = x_ref[...]
    h = _rms(x, nw_ref[...]).astype(BF16)
    row = lax.broadcasted_iota(jnp.int32, (tm, chunk), 0)
    for c in range(d_ff // chunk):
        cols = slice(c * chunk, (c + 1) * chunk)
        g = _dot(h, wup_ref[:, cols])
        v = _dot(h, wup_ref[:, d_ff + c * chunk:d_ff + (c + 1) * chunk])
        p1 = halo_ref[SUBLANES - 1:SUBLANES, cols]
        p2 = halo_ref[SUBLANES - 2:SUBLANES - 1, cols]
        gm1 = jnp.where(row == 0, p1, pltpu.roll(g, 1, 0))
        gm2 = jnp.where(row == 0, p2, jnp.where(row == 1, p1, pltpu.roll(g, 2, 0)))
        halo_ref[:, cols] = g[tm - SUBLANES:tm, :]
        y = (cb_ref[0:1, cols] + cw_ref[0:1, cols] * gm2 + cw_ref[1:2, cols] * gm1
             + cw_ref[2:3, cols] * g)
        act_ref[:, cols] = (y * jax.nn.sigmoid(y) * v).astype(BF16)
    out = x + _dot(act_ref[...], wdn_ref[...])
    if final_norm:
        out = _rms(out, fnw_ref[...])
    o_ref[...] = out


def _ffn(x3, layer, norm_w, w_up, conv_w, conv_b, w_down, final_w, final_norm, tm, chunk):
    b, l, d = x3.shape
    d_ff = w_down.shape[1]
    kern = functools.partial(_ffn_kernel, d_ff=d_ff, chunk=chunk, final_norm=final_norm)
    blk = pl.BlockSpec((None, tm, d), lambda bi, i: (bi, i, 0))
    return pl.pallas_call(
        kern,
        grid=(b, l // tm),
        in_specs=[blk] + [_layer_spec(a, layer) for a in (norm_w, w_up, conv_w, conv_b, w_down)]
        + [_layer_spec(final_w, 0)],
        out_specs=blk,
        out_shape=jax.ShapeDtypeStruct((b, l, d), F32),
        scratch_shapes=[pltpu.VMEM((SUBLANES, d_ff), F32),
                        pltpu.VMEM((tm, d_ff), BF16)],
        compiler_params=_cparams(2, True),
        name="conv_ffn",
    )(x3, norm_w, w_up, conv_w, conv_b, w_down, final_w)


def _s5_params(lam_re, lam_im, log_dt, b_re, b_im, c_re, c_im, seg_len):
    g, p = lam_re.shape
    cg = b_re.shape[2]
    lr = jnp.minimum(lam_re.astype(F32), -1e-4)
    li = lam_im.astype(F32)
    dt = jnp.exp(log_dt.astype(F32))[:, None]
    mag = jnp.exp(lr * dt)
    ab_re = mag * jnp.cos(li * dt)
    ab_im = mag * jnp.sin(li * dt)
    den = lr * lr + li * li
    f_re = ((ab_re - 1.0) * lr + ab_im * li) / den
    f_im = (ab_im * lr - (ab_re - 1.0) * li) / den
    br = b_re.astype(F32)
    bi = b_im.astype(F32)
    bb_re = f_re[..., None] * br - f_im[..., None] * bi
    bb_im = f_re[..., None] * bi + f_im[..., None] * br
    eye = jnp.eye(g, dtype=F32)

    def in_blk(bb):
        return jnp.einsum('gpc,gh->gchp', bb, eye).reshape(g * cg, g * p)

    def out_blk(cc):
        return jnp.einsum('gcp,gh->gphc', cc, eye).reshape(g * p, g * cg)

    bblk = jnp.concatenate([in_blk(bb_re), in_blk(bb_im)], axis=1).astype(BF16)
    cblk = jnp.concatenate([out_blk(c_re.astype(F32)), -out_blk(c_im.astype(F32))],
                           axis=0).astype(BF16)
    seg_mag = jnp.exp(seg_len * (lr * dt))
    seg_ang = seg_len * (li * dt)
    lampow = jnp.stack([
        jnp.concatenate([ab_re.reshape(-1), ab_im.reshape(-1)]),
        jnp.concatenate([(seg_mag * jnp.cos(seg_ang)).reshape(-1),
                         (seg_mag * jnp.sin(seg_ang)).reshape(-1)])])
    return bblk, lampow, cblk


def _rows(v, width=None):
    v = v.astype(F32)[:, None, :]
    if width is not None:
        v = jnp.pad(v, ((0, 0), (0, 0), (0, width - v.shape[2])))
    return v


def kernel(x, mem, mix_norm_w, w_in, s5_lambda_re, s5_lambda_im, s5_log_dt, s5_b_re, s5_b_im,
           s5_c_re, s5_c_im, s5_d, s5_w_glu, ssd_conv_w, ssd_conv_b, ssd_dt_bias, ssd_a_log,
           ssd_d, ssd_norm_w, w_out, xa_norm_w, mem_norm_w, xa_wq, xa_wk, xa_wv, xa_wo,
           ffn_norm_w, ffn_w_up, ffn_conv_w, ffn_conv_b, ffn_w_down, final_norm_w):
    bsz, seq, d = x.shape
    depth = w_in.shape[0]
    s5_w = s5_d.shape[1]
    inner = ssd_norm_w.shape[1]
    n_heads = ssd_a_log.shape[1]
    cdim = ssd_conv_w.shape[2]
    c_u, c_z, c_xbc = s5_w, s5_w + inner, s5_w + inner + cdim
    in_cols = w_in.shape[2]

    tri = jnp.tril(jnp.ones((1, SSD_CHUNK, SSD_CHUNK), F32)).astype(BF16)
    tri = jnp.broadcast_to(tri, (depth, SSD_CHUNK, SSD_CHUNK))
    w_in_p = jnp.pad(w_in, ((0, 0), (0, 0), (0, c_xbc + LANES - in_cols))).astype(BF16)
    bblk, lampow, cblk = jax.vmap(functools.partial(_s5_params, seg_len=TM_MIX // SUBLANES))(
        s5_lambda_re, s5_lambda_im, s5_log_dt, s5_b_re, s5_b_im, s5_c_re, s5_c_im)
    s5p = (bblk, lampow, cblk, _rows(s5_d), s5_w_glu.astype(BF16))
    ssdp = (ssd_conv_w.astype(F32), _rows(ssd_conv_b), _rows(ssd_dt_bias, LANES),
            _rows(ssd_a_log, LANES), _rows(jnp.repeat(ssd_d, SSD_HEAD_DIM, axis=1)),
            _rows(ssd_norm_w), tri)
    w_out_bf = w_out.astype(BF16)
    mix_nw, xa_nw, ffn_nw = _rows(mix_norm_w), _rows(xa_norm_w), _rows(ffn_norm_w)
    w_up_bf, w_down_bf = ffn_w_up.astype(BF16), ffn_w_down.astype(BF16)
    ffn_cw, ffn_cb = ffn_conv_w.astype(F32), _rows(ffn_conv_b)
    final_w = final_norm_w.astype(F32).reshape(1, 1, d)
    ws_all, wvo_all = _mem_kv(mem, _rows(mem_norm_w), xa_wk.astype(BF16),
                              xa_wv.astype(BF16), xa_wq.astype(BF16), xa_wo.astype(BF16))

    for i in range(depth):
        x = _mixer(x, i, mix_nw, w_in_p, s5p, ssdp, w_out_bf, c_u, c_z, c_xbc, n_heads, TM_MIX)
        x = _xattn(x, i, xa_nw, ws_all, wvo_all, TM_XA)
        x = _ffn(x, i, ffn_nw, w_up_bf, ffn_cw, ffn_cb, w_down_bf, final_w,
                 i == depth - 1, TM_FFN, FFN_CHUNK)
    return x
```

```python
import functools

import jax
import jax.numpy as jnp
from jax import lax
from jax.experimental import pallas as pl
from jax.experimental.pallas import tpu as pltpu

F32 = jnp.float32
BF16 = jnp.bfloat16

NORM_EPS = 1e-6
LANES = 128
SUBLANES = 8
VMEM_LIMIT = 56 * 1024 * 1024

SSD_HEAD_DIM = 64
SSD_GROUPS = 4
SSD_STATE = 128
SSD_CONV = 4
SSD_CHUNK = 128
XA_HEADS = 4

TM_MIX = 512
TM_XA = 512
TM_FFN = 512
FFN_CHUNK = 256


def _cparams(n_axes, sequential):
    sem = ("arbitrary" if sequential else "parallel",) * n_axes
    return pltpu.CompilerParams(dimension_semantics=sem, vmem_limit_bytes=VMEM_LIMIT)


def _layer_spec(stacked, layer):
    nd = stacked.ndim - 1
    return pl.BlockSpec((None,) + stacked.shape[1:], lambda *_: (layer,) + (0,) * nd,
                        pipeline_mode=pl.Buffered(1))


def _rms(xf, w):
    ms = jnp.mean(xf * xf, axis=-1, keepdims=True)
    return xf * lax.rsqrt(ms + NORM_EPS) * w


def _dot(a, b):
    return jnp.dot(a, b, preferred_element_type=F32)


def _split3(x):
    hi = x.astype(BF16)
    r1 = x - hi.astype(F32)
    mid = r1.astype(BF16)
    lo = (r1 - mid.astype(F32)).astype(BF16)
    return hi, mid, lo


def _s5_gather(u_ref):
    nw, tm, _ = u_ref.shape
    r_len = tm // SUBLANES
    u_perm = jnp.concatenate(
        [jnp.concatenate([u_ref[w, pl.ds(r, SUBLANES, stride=r_len), :] for w in range(nw)],
                         axis=1) for r in range(r_len)], axis=0)
    return u_perm, jnp.concatenate([u_ref[w] for w in range(nw)], axis=1)


def _s5_scan(u_perm, bblk_ref, lampow_ref, cblk_ref, st_ref, carry_ref, cin_ref,
             pump, pump_every):
    tm = u_perm.shape[0]
    r_len = tm // SUBLANES
    ns = bblk_ref.shape[1] // 2
    nt = ns // LANES

    st_ref[...] = _dot(u_perm.astype(BF16), bblk_ref[...])

    def lanes(ref, r0, r1, j):
        return ref[r0:r1, j * LANES:(j + 1) * LANES]

    a_re = [jnp.broadcast_to(lanes(lampow_ref, 0, 1, j), (SUBLANES, LANES)) for j in range(nt)]
    a_im = [jnp.broadcast_to(lanes(lampow_ref, 0, 1, nt + j), (SUBLANES, LANES))
            for j in range(nt)]

    def scan(h_re, h_im, keep):
        for r in range(r_len):
            r0, r1 = r * SUBLANES, (r + 1) * SUBLANES
            for j in range(nt):
                n_re = a_re[j] * h_re[j] - a_im[j] * h_im[j] + lanes(st_ref, r0, r1, j)
                n_im = a_re[j] * h_im[j] + a_im[j] * h_re[j] + lanes(st_ref, r0, r1, nt + j)
                h_re[j], h_im[j] = n_re, n_im
                if keep:
                    st_ref[r0:r1, j * LANES:(j + 1) * LANES] = n_re
                    st_ref[r0:r1, (nt + j) * LANES:(nt + j + 1) * LANES] = n_im
            if r % pump_every == pump_every - 1:
                pump()
        return h_re, h_im

    h_re, h_im = scan([jnp.zeros((SUBLANES, LANES), F32) for _ in range(nt)],
                      [jnp.zeros((SUBLANES, LANES), F32) for _ in range(nt)], keep=False)

    for j in range(nt):
        q_re = lanes(lampow_ref, 1, 2, j)
        q_im = lanes(lampow_ref, 1, 2, nt + j)
        c_re = lanes(carry_ref, 0, 1, j)
        c_im = lanes(carry_ref, 0, 1, nt + j)
        for s in range(SUBLANES):
            cin_ref[s:s + 1, j * LANES:(j + 1) * LANES] = c_re
            cin_ref[s:s + 1, (nt + j) * LANES:(nt + j + 1) * LANES] = c_im
            e_re = h_re[j][s:s + 1, :]
            e_im = h_im[j][s:s + 1, :]
            c_re, c_im = (e_re + q_re * c_re - q_im * c_im,
                          e_im + q_re * c_im + q_im * c_re)
        carry_ref[0:1, j * LANES:(j + 1) * LANES] = c_re
        carry_ref[0:1, (nt + j) * LANES:(nt + j + 1) * LANES] = c_im

    scan([lanes(cin_ref, 0, SUBLANES, j) for j in range(nt)],
         [lanes(cin_ref, 0, SUBLANES, nt + j) for j in range(nt)], keep=True)

    half = tm // 2
    return jnp.concatenate([_dot(st_ref[0:half, :].astype(BF16), cblk_ref[...]),
                            _dot(st_ref[half:tm, :].astype(BF16), cblk_ref[...])], axis=0)


def _s5_finish(yp, u, d_ref, wglu_ref, yp_ref):
    tm, width = u.shape
    nw = width // LANES
    r_len = tm // SUBLANES
    for r in range(r_len):
        for w in range(nw):
            yp_ref[w, pl.ds(r, SUBLANES, stride=r_len), :] = (
                yp[r * SUBLANES:(r + 1) * SUBLANES, w * LANES:(w + 1) * LANES])
    y = jnp.concatenate([yp_ref[w] for w in range(nw)], axis=1) + u * d_ref[...]
    y = jax.nn.gelu(y)
    gate = jax.nn.sigmoid(_dot(y.astype(BF16), wglu_ref[...]))
    return y * gate


def _ssd_conv(cat_ref, cw_ref, cb_ref):
    halo = SUBLANES
    tm = cat_ref.shape[0] - halo
    acc = cb_ref[...] + cw_ref[SSD_CONV - 1:SSD_CONV, :] * cat_ref[halo:halo + tm, :]
    for k in range(1, SSD_CONV):
        acc = acc + (cw_ref[SSD_CONV - 1 - k:SSD_CONV - k, :]
                     * cat_ref[pl.ds(halo - k, tm), :])
    cat_ref[0:halo, :] = cat_ref[tm:tm + halo, :]
    return acc * jax.nn.sigmoid(acc)


def _ssd_body(xbc, dtv, alog_ref, dsk_ref, tri_ref, state_ref, pump, emit, *, n_heads, inner):
    tm = xbc.shape[0]
    q = SSD_CHUNK
    hd = SSD_HEAD_DIM
    hpg = n_heads // SSD_GROUPS
    gs = SSD_GROUPS * SSD_STATE

    a_neg = -jnp.exp(alog_ref[...])
    da = dtv * a_neg
    tri = tri_ref[...]
    li = lax.broadcasted_iota(jnp.int32, (q, q), 0)
    si = lax.broadcasted_iota(jnp.int32, (q, q), 1)
    causal = li >= si

    hpt = LANES // hd
    n_tiles = n_heads // hpt
    win = [(g * hpg * hd) // LANES for g in range(SSD_GROUPS)]
    assert all((g + 1) * hpg * hd <= (win[g] + 2) * LANES for g in range(SSD_GROUPS))
    first_head = lax.broadcasted_iota(jnp.int32, (q, LANES), 1) < hd
    first_head_row = lax.broadcasted_iota(jnp.int32, (1, LANES), 1) < hd
    win_head = lax.broadcasted_iota(jnp.int32, (q, 2 * LANES), 1) // hd

    def expand(src, rows, sel):
        return [jnp.where(sel,
                          jnp.broadcast_to(src[:, hpt * v:hpt * v + 1], (rows, LANES)),
                          jnp.broadcast_to(src[:, hpt * v + 1:hpt * v + 2], (rows, LANES)))
                for v in range(n_tiles)]

    def window(tiles, g):
        return jnp.concatenate([tiles[win[g]], tiles[win[g] + 1]], axis=1)

    n_chunks = tm // q
    groups = range(SSD_GROUPS)

    a_cs_all, a_cs_t_all = [], []
    for c in range(n_chunks):
        hi, mid, lo = _split3(da[c * q:(c + 1) * q, :])
        a_cs = _dot(tri, hi) + _dot(tri, mid) + _dot(tri, lo)
        a_cs_all.append(a_cs)
        a_cs_t_all.append(a_cs.T)

    for c in range(n_chunks):
        r0 = c * q
        a_cs, a_cs_t = a_cs_all[c], a_cs_t_all[c]
        dt_c = dtv[r0:r0 + q, :]
        dt_t = dt_c.T
        a_last = a_cs[q - 1:q, :]
        lanes_w = [slice(win[g] * LANES, (win[g] + 2) * LANES) for g in groups]
        x_w = [xbc[r0:r0 + q, lanes_w[g]] for g in groups]
        x_bf = [x_w[g].astype(BF16) for g in groups]
        b_bf = [xbc[r0:r0 + q, inner + g * SSD_STATE:inner + (g + 1) * SSD_STATE].astype(BF16)
                for g in groups]
        c_bf = [xbc[r0:r0 + q, inner + gs + g * SSD_STATE:inner + gs + (g + 1) * SSD_STATE]
                .astype(BF16) for g in groups]
        cb = [lax.dot_general(c_bf[g], b_bf[g], (((1,), (1,)), ((), ())),
                              preferred_element_type=F32) for g in groups]
        s_prev = [state_ref[g] for g in groups]
        y_off = [_dot(c_bf[g], s_prev[g].astype(BF16)) for g in groups]
        e_t = expand(jnp.exp(a_cs), q, first_head)
        w_t = expand(jnp.exp(a_last - a_cs) * dt_c, q, first_head)
        d_t = expand(jnp.exp(a_last), 1, first_head_row)
        m_all = []
        for h in range(n_heads):
            col = a_cs[:, h:h + 1]
            row = a_cs_t[h:h + 1, :]
            lmat = jnp.exp(jnp.where(causal, col - row, -jnp.inf))
            m_all.append((cb[h // hpg] * lmat * dt_t[h:h + 1, :]).astype(BF16))
            if h % hpg == hpg - 1 and (h // hpg) % 2 == 1:
                pump()
        y_win = []
        for g in groups:
            y_diag = None
            for j in range(hpg):
                h = g * hpg + j
                r_j = _dot(m_all[h], x_bf[g])
                y_diag = r_j if y_diag is None else jnp.where(
                    win_head == h - win[g] * hpt, r_j, y_diag)
            xw = (x_w[g] * window(w_t, g)).astype(BF16)
            upd = lax.dot_general(b_bf[g], xw, (((0,), (0,)), ((), ())),
                                  preferred_element_type=F32)
            state_ref[g] = s_prev[g] * window(d_t, g) + upd
            y_win.append(y_diag + y_off[g] * window(e_t, g) + x_w[g] * dsk_ref[0:1, lanes_w[g]])
        tiles = []
        for v in range(n_tiles):
            ga, gb = (hpt * v) // hpg, (hpt * v + 1) // hpg
            ta = y_win[ga][:, (v - win[ga]) * LANES:(v - win[ga] + 1) * LANES]
            if ga == gb:
                tiles.append(ta)
            else:
                tb = y_win[gb][:, (v - win[gb]) * LANES:(v - win[gb] + 1) * LANES]
                tiles.append(jnp.where(first_head, ta, tb))
        emit(c, jnp.concatenate(tiles, axis=1))


def _proj_items(h, win_ref, u_ref, cat_ref, z_ref, dt_ref, c_u, c_z, c_xbc, width):
    halo = SUBLANES
    tm = h.shape[0]

    def u_job():
        u = _dot(h, win_ref[:, 0:c_u])
        for w in range(c_u // LANES):
            u_ref[w] = u[:, w * LANES:(w + 1) * LANES]

    def col_job(dst_ref, r0, c_src, c_dst, n):
        def job():
            dst_ref[r0:r0 + tm, c_dst:c_dst + n] = _dot(h, win_ref[:, c_src:c_src + n])
        return job

    jobs = [u_job, col_job(dt_ref, 0, c_xbc, 0, LANES)]
    for c0 in range(0, c_z - c_u, width):
        jobs.append(col_job(z_ref, 0, c_u + c0, c0, min(width, c_z - c_u - c0)))
    n_early = len(jobs)
    for c0 in range(0, c_xbc - c_z, width):
        jobs.append(col_job(cat_ref, halo, c_z + c0, c0, min(width, c_xbc - c_z - c0)))
    return jobs, n_early


def _mixer_kernel(x_ref, xn_ref, nw_ref, win_ref,
                  bblk_ref, lampow_ref, cblk_ref, s5d_ref, wglu_ref,
                  cw_ref, cb_ref, dtb_ref, alog_ref, dsk_ref, snw_ref, tri_ref,
                  wo_ref, o_ref,
                  u_ref, z_ref, dt_ref, st_ref, yp_ref, carry_ref, cin_ref, cat_ref, state_ref,
                  *, c_u, c_z, c_xbc, n_heads):
    halo = SUBLANES
    q = SSD_CHUNK
    proj = functools.partial(_proj_items, win_ref=win_ref, u_ref=u_ref, cat_ref=cat_ref,
                             z_ref=z_ref, dt_ref=dt_ref, c_u=c_u, c_z=c_z, c_xbc=c_xbc,
                             width=2 * LANES)

    @pl.when(pl.program_id(1) == 0)
    def _():
        carry_ref[...] = jnp.zeros_like(carry_ref)
        cat_ref[0:halo, :] = jnp.zeros((halo, cat_ref.shape[1]), F32)
        state_ref[...] = jnp.zeros_like(state_ref)
        for job in proj(_rms(x_ref[...], nw_ref[...]).astype(BF16))[0]:
            job()

    u_perm, u = _s5_gather(u_ref)
    zz = z_ref[...]
    z_gate = zz * jax.nn.sigmoid(zz)
    dtx = dt_ref[...] + dtb_ref[...]
    dtv = jnp.maximum(dtx, 0.0) + jnp.log(1.0 + jnp.exp(-jnp.abs(dtx)))

    jobs, n_early = proj(_rms(xn_ref[...], nw_ref[...]).astype(BF16))
    early, late = jobs[:n_early], jobs[n_early:]

    def pump_early():
        if early:
            early.pop(0)()

    def pump_late():
        if late:
            late.pop(0)()

    yp = _s5_scan(u_perm, bblk_ref, lampow_ref, cblk_ref, st_ref, carry_ref, cin_ref,
                  pump_early, pump_every=32)
    xbc = _ssd_conv(cat_ref, cw_ref, cb_ref)
    while early:
        pump_early()
    y_s5 = _s5_finish(yp, u, s5d_ref, wglu_ref, yp_ref)
    o_ref[...] = x_ref[...] + _dot(y_s5.astype(BF16), wo_ref[0:c_u, :])

    def emit(c, y_c):
        rows = slice(c * q, (c + 1) * q)
        y_n = _rms(y_c * z_gate[rows, :], snw_ref[...]).astype(BF16)
        o_ref[rows, :] = o_ref[rows, :] + _dot(y_n, wo_ref[c_u:c_z, :])

    _ssd_body(xbc, dtv, alog_ref, dsk_ref, tri_ref, state_ref, pump_late, emit,
              n_heads=n_heads, inner=c_z - c_u)
    while late:
        pump_late()


def _mixer(x3, layer, norm_w, w_in_p, s5p, ssdp, w_out, c_u, c_z, c_xbc, n_heads, tm):
    b, l, d = x3.shape
    ns2 = s5p[0].shape[2]
    inner = c_z - c_u
    cdim = c_xbc - c_z
    n_t = l // tm
    consts = (norm_w, w_in_p) + tuple(s5p) + tuple(ssdp) + (w_out,)
    kern = functools.partial(_mixer_kernel, c_u=c_u, c_z=c_z, c_xbc=c_xbc, n_heads=n_heads)
    blk = pl.BlockSpec((None, tm, d), lambda bi, i: (bi, i, 0))
    nxt = pl.BlockSpec((None, tm, d), lambda bi, i: (bi, jnp.minimum(i + 1, n_t - 1), 0))
    return pl.pallas_call(
        kern,
        grid=(b, n_t),
        in_specs=[blk, nxt] + [_layer_spec(a, layer) for a in consts],
        out_specs=blk,
        out_shape=jax.ShapeDtypeStruct((b, l, d), F32),
        scratch_shapes=[pltpu.VMEM((c_u // LANES, tm, LANES), F32),
                        pltpu.VMEM((tm, inner), F32),
                        pltpu.VMEM((tm, LANES), F32),
                        pltpu.VMEM((tm, ns2), F32),
                        pltpu.VMEM((c_u // LANES, tm, LANES), F32),
                        pltpu.VMEM((SUBLANES, ns2), F32),
                        pltpu.VMEM((SUBLANES, ns2), F32),
                        pltpu.VMEM((tm + SUBLANES, cdim), F32),
                        pltpu.VMEM((SSD_GROUPS, SSD_STATE, 2 * LANES), F32)],
        compiler_params=_cparams(2, True),
        name="mixer",
    )(x3, x3, *consts)


def _kv_kernel(mem_ref, nw_ref, wk_ref, wv_ref, wq_ref, wo_ref, ws_ref, wvo_ref):
    m = _rms(mem_ref[...], nw_ref[...]).astype(BF16)
    n_mem, d = mem_ref.shape
    k = _dot(m, wk_ref[...]).astype(BF16)
    v = _dot(m, wv_ref[...]).astype(BF16)
    hd = d // XA_HEADS
    scale = hd ** -0.5
    for hh in range(XA_HEADS):
        cols = slice(hh * hd, (hh + 1) * hd)
        mcols = slice(hh * n_mem, (hh + 1) * n_mem)
        qk = lax.dot_general(wq_ref[:, cols], k[:, cols], (((1,), (1,)), ((), ())),
                             preferred_element_type=F32)
        ws_ref[:, mcols] = (qk * scale).astype(ws_ref.dtype)
        wvo_ref[mcols, :] = _dot(v[:, cols], wo_ref[cols, :]).astype(wvo_ref.dtype)


def _mem_kv(mem, mem_norm_w, wk, wv, wq, wo):
    b, m, d = mem.shape
    nl = wk.shape[0]
    hm = XA_HEADS * m
    wspec = pl.BlockSpec((None, d, d), lambda li, bi: (li, 0, 0))
    return pl.pallas_call(
        _kv_kernel,
        grid=(nl, b),
        in_specs=[pl.BlockSpec((None, m, d), lambda li, bi: (bi, 0, 0)),
                  pl.BlockSpec((None, 1, d), lambda li, bi: (li, 0, 0)),
                  wspec, wspec, wspec, wspec],
        out_specs=[pl.BlockSpec((None, None, d, hm), lambda li, bi: (li, bi, 0, 0)),
                   pl.BlockSpec((None, None, hm, d), lambda li, bi: (li, bi, 0, 0))],
        out_shape=[jax.ShapeDtypeStruct((nl, b, d, hm), BF16),
                   jax.ShapeDtypeStruct((nl, b, hm, d), BF16)],
        compiler_params=_cparams(2, False),
        name="mem_kv",
    )(mem, mem_norm_w, wk, wv, wq, wo)


def _xattn_kernel(x_ref, nw_ref, ws_ref, wvo_ref, o_ref, p_ref):
    x = x_ref[...]
    h = _rms(x, nw_ref[...]).astype(BF16)
    s_all = _dot(h, ws_ref[...])
    n_mem = ws_ref.shape[1] // XA_HEADS
    for hh in range(XA_HEADS):
        mcols = slice(hh * n_mem, (hh + 1) * n_mem)
        s = s_all[:, mcols]
        s = s - jnp.max(s, axis=-1, keepdims=True)
        p = jnp.exp(s)
        p = p / jnp.sum(p, axis=-1, keepdims=True)
        p_ref[:, mcols] = p.astype(p_ref.dtype)
    o_ref[...] = x + _dot(p_ref[...], wvo_ref[...])


def _xattn(x3, layer, norm_w, ws, wvo, tm):
    b, l, d = x3.shape
    hm = ws.shape[3]
    blk = pl.BlockSpec((None, tm, d), lambda bi, i: (bi, i, 0))
    return pl.pallas_call(
        _xattn_kernel,
        grid=(b, l // tm),
        in_specs=[blk, _layer_spec(norm_w, layer),
                  pl.BlockSpec((None, None, d, hm), lambda bi, i: (layer, bi, 0, 0)),
                  pl.BlockSpec((None, None, hm, d), lambda bi, i: (layer, bi, 0, 0))],
        out_specs=blk,
        out_shape=jax.ShapeDtypeStruct((b, l, d), F32),
        scratch_shapes=[pltpu.VMEM((tm, hm), BF16)],
        compiler_params=_cparams(2, False),
        name="xattn",
    )(x3, norm_w, ws, wvo)


def _ffn_kernel(x_ref, nw_ref, wup_ref, cw_ref, cb_ref, wdn_ref, fnw_ref, o_ref,
                halo_ref, act_ref, *, d_ff, chunk, final_norm):
    tm = x_ref.shape[0]

    @pl.when(pl.program_id(1) == 0)
    def _():
        halo_ref[...] = jnp.zeros_like(halo_ref)

    x = x_ref[...]
    h = _rms(x, nw_ref[...]).astype(BF16)
    row = lax.broadcasted_iota(jnp.int32, (tm, chunk), 0)
    for c in range(d_ff // chunk):
        cols = slice(c * chunk, (c + 1) * chunk)
        g = _dot(h, wup_ref[:, cols])
        v = _dot(h, wup_ref[:, d_ff + c * chunk:d_ff + (c + 1) * chunk])
        p1 = halo_ref[SUBLANES - 1:SUBLANES, cols]
        p2 = halo_ref[SUBLANES - 2:SUBLANES - 1, cols]
        gm1 = jnp.where(row == 0, p1, pltpu.roll(g, 1, 0))
        gm2 = jnp.where(row == 0, p2, jnp.where(row == 1, p1, pltpu.roll(g, 2, 0)))
        halo_ref[:, cols] = g[tm - SUBLANES:tm, :]
        y = (cb_ref[0:1, cols] + cw_ref[0:1, cols] * gm2 + cw_ref[1:2, cols] * gm1
             + cw_ref[2:3, cols] * g)
        act_ref[:, cols] = (y * jax.nn.sigmoid(y) * v).astype(BF16)
    out = x + _dot(act_ref[...], wdn_ref[...])
    if final_norm:
        out = _rms(out, fnw_ref[...])
    o_ref[...] = out


def _ffn(x3, layer, norm_w, w_up, conv_w, conv_b, w_down, final_w, final_norm, tm, chunk):
    b, l, d = x3.shape
    d_ff = w_down.shape[1]
    kern = functools.partial(_ffn_kernel, d_ff=d_ff, chunk=chunk, final_norm=final_norm)
    blk = pl.BlockSpec((None, tm, d), lambda bi, i: (bi, i, 0))
    return pl.pallas_call(
        kern,
        grid=(b, l // tm),
        in_specs=[blk] + [_layer_spec(a, layer) for a in (norm_w, w_up, conv_w, conv_b, w_down)]
        + [_layer_spec(final_w, 0)],
        out_specs=blk,
        out_shape=jax.ShapeDtypeStruct((b, l, d), F32),
        scratch_shapes=[pltpu.VMEM((SUBLANES, d_ff), F32),
                        pltpu.VMEM((tm, d_ff), BF16)],
        compiler_params=_cparams(2, True),
        name="conv_ffn",
    )(x3, norm_w, w_up, conv_w, conv_b, w_down, final_w)


def _s5_params(lam_re, lam_im, log_dt, b_re, b_im, c_re, c_im, seg_len):
    g, p = lam_re.shape
    cg = b_re.shape[2]
    lr = jnp.minimum(lam_re.astype(F32), -1e-4)
    li = lam_im.astype(F32)
    dt = jnp.exp(log_dt.astype(F32))[:, None]
    mag = jnp.exp(lr * dt)
    ab_re = mag * jnp.cos(li * dt)
    ab_im = mag * jnp.sin(li * dt)
    den = lr * lr + li * li
    f_re = ((ab_re - 1.0) * lr + ab_im * li) / den
    f_im = (ab_im * lr - (ab_re - 1.0) * li) / den
    br = b_re.astype(F32)
    bi = b_im.astype(F32)
    bb_re = f_re[..., None] * br - f_im[..., None] * bi
    bb_im = f_re[..., None] * bi + f_im[..., None] * br
    eye = jnp.eye(g, dtype=F32)

    def in_blk(bb):
        return jnp.einsum('gpc,gh->gchp', bb, eye).reshape(g * cg, g * p)

    def out_blk(cc):
        return jnp.einsum('gcp,gh->gphc', cc, eye).reshape(g * p, g * cg)

    bblk = jnp.concatenate([in_blk(bb_re), in_blk(bb_im)], axis=1).astype(BF16)
    cblk = jnp.concatenate([out_blk(c_re.astype(F32)), -out_blk(c_im.astype(F32))],
                           axis=0).astype(BF16)
    seg_mag = jnp.exp(seg_len * (lr * dt))
    seg_ang = seg_len * (li * dt)
    lampow = jnp.stack([
        jnp.concatenate([ab_re.reshape(-1), ab_im.reshape(-1)]),
        jnp.concatenate([(seg_mag * jnp.cos(seg_ang)).reshape(-1),
                         (seg_mag * jnp.sin(seg_ang)).reshape(-1)])])
    return bblk, lampow, cblk


def _rows(v, width=None):
    v = v.astype(F32)[:, None, :]
    if width is not None:
        v = jnp.pad(v, ((0, 0), (0, 0), (0, width - v.shape[2])))
    return v


def kernel(x, mem, mix_norm_w, w_in, s5_lambda_re, s5_lambda_im, s5_log_dt, s5_b_re, s5_b_im,
           s5_c_re, s5_c_im, s5_d, s5_w_glu, ssd_conv_w, ssd_conv_b, ssd_dt_bias, ssd_a_log,
           ssd_d, ssd_norm_w, w_out, xa_norm_w, mem_norm_w, xa_wq, xa_wk, xa_wv, xa_wo,
           ffn_norm_w, ffn_w_up, ffn_conv_w, ffn_conv_b, ffn_w_down, final_norm_w):
    bsz, seq, d = x.shape
    depth = w_in.shape[0]
    s5_w = s5_d.shape[1]
    inner = ssd_norm_w.shape[1]
    n_heads = ssd_a_log.shape[1]
    cdim = ssd_conv_w.shape[2]
    c_u, c_z, c_xbc = s5_w, s5_w + inner, s5_w + inner + cdim
    in_cols = w_in.shape[2]

    tri = jnp.tril(jnp.ones((1, SSD_CHUNK, SSD_CHUNK), F32)).astype(BF16)
    tri = jnp.broadcast_to(tri, (depth, SSD_CHUNK, SSD_CHUNK))
    w_in_p = jnp.pad(w_in, ((0, 0), (0, 0), (0, c_xbc + LANES - in_cols))).astype(BF16)
    bblk, lampow, cblk = jax.vmap(functools.partial(_s5_params, seg_len=TM_MIX // SUBLANES))(
        s5_lambda_re, s5_lambda_im, s5_log_dt, s5_b_re, s5_b_im, s5_c_re, s5_c_im)
    s5p = (bblk, lampow, cblk, _rows(s5_d), s5_w_glu.astype(BF16))
    ssdp = (ssd_conv_w.astype(F32), _rows(ssd_conv_b), _rows(ssd_dt_bias, LANES),
            _rows(ssd_a_log, LANES), _rows(jnp.repeat(ssd_d, SSD_HEAD_DIM, axis=1)),
            _rows(ssd_norm_w), tri)
    w_out_bf = w_out.astype(BF16)
    mix_nw, xa_nw, ffn_nw = _rows(mix_norm_w), _rows(xa_norm_w), _rows(ffn_norm_w)
    w_up_bf, w_down_bf = ffn_w_up.astype(BF16), ffn_w_down.astype(BF16)
    ffn_cw, ffn_cb = ffn_conv_w.astype(F32), _rows(ffn_conv_b)
    final_w = final_norm_w.astype(F32).reshape(1, 1, d)
    ws_all, wvo_all = _mem_kv(mem, _rows(mem_norm_w), xa_wk.astype(BF16),
                              xa_wv.astype(BF16), xa_wq.astype(BF16), xa_wo.astype(BF16))

    for i in range(depth):
        x = _mixer(x, i, mix_nw, w_in_p, s5p, ssdp, w_out_bf, c_u, c_z, c_xbc, n_heads, TM_MIX)
        x = _xattn(x, i, xa_nw, ws_all, wvo_all, TM_XA)
        x = _ffn(x, i, ffn_nw, w_up_bf, ffn_cw, ffn_cb, w_down_bf, final_w,
                 i == depth - 1, TM_FFN, FFN_CHUNK)
    return x
```

```python
import functools

import jax
import jax.numpy as jnp
from jax import lax
from jax.experimental import pallas as pl
from jax.experimental.pallas import tpu as pltpu

F32 = jnp.float32
BF16 = jnp.bfloat16

NORM_EPS = 1e-6
LANES = 128
SUBLANES = 8
VMEM_LIMIT = 56 * 1024 * 1024

SSD_HEAD_DIM = 64
SSD_GROUPS = 4
SSD_STATE = 128
SSD_CONV = 4
SSD_CHUNK = 128
XA_HEADS = 4

TM_MIX = 512
TM_XA = 1024
TM_FFN = 512
FFN_SUB = 2
FFN_CHUNK = 256


def _cparams(n_axes, sequential):
    sem = ("arbitrary" if sequential else "parallel",) * n_axes
    return pltpu.CompilerParams(dimension_semantics=sem, vmem_limit_bytes=VMEM_LIMIT)


def _layer_spec(stacked, layer):
    nd = stacked.ndim - 1
    return pl.BlockSpec((None,) + stacked.shape[1:], lambda *_: (layer,) + (0,) * nd,
                        pipeline_mode=pl.Buffered(1))


def _rms(xf, w):
    ms = jnp.mean(xf * xf, axis=-1, keepdims=True)
    return xf * lax.rsqrt(ms + NORM_EPS) * w


def _dot(a, b):
    return jnp.dot(a, b, preferred_element_type=F32)


def _split3(x):
    hi = x.astype(BF16)
    r1 = x - hi.astype(F32)
    mid = r1.astype(BF16)
    lo = (r1 - mid.astype(F32)).astype(BF16)
    return hi, mid, lo


def _s5_gather(u_ref):
    nw, tm, _ = u_ref.shape
    r_len = tm // SUBLANES
    return jnp.concatenate(
        [jnp.concatenate([u_ref[w, pl.ds(r, SUBLANES, stride=r_len), :] for w in range(nw)],
                         axis=1) for r in range(r_len)], axis=0)


def _s5_scan(u_perm, bblk_ref, lampow_ref, cblk_ref, st_ref, carry_ref, cin_ref,
             pump, pump_every):
    tm = u_perm.shape[0]
    r_len = tm // SUBLANES
    ns = bblk_ref.shape[1] // 2
    nt = ns // LANES

    st_ref[...] = _dot(u_perm.astype(BF16), bblk_ref[...])

    def lanes(ref, r0, r1, j):
        return ref[r0:r1, j * LANES:(j + 1) * LANES]

    a_re = [jnp.broadcast_to(lanes(lampow_ref, 0, 1, j), (SUBLANES, LANES)) for j in range(nt)]
    a_im = [jnp.broadcast_to(lanes(lampow_ref, 0, 1, nt + j), (SUBLANES, LANES))
            for j in range(nt)]

    def scan(h_re, h_im, keep):
        for r in range(r_len):
            r0, r1 = r * SUBLANES, (r + 1) * SUBLANES
            for j in range(nt):
                n_re = a_re[j] * h_re[j] - a_im[j] * h_im[j] + lanes(st_ref, r0, r1, j)
                n_im = a_re[j] * h_im[j] + a_im[j] * h_re[j] + lanes(st_ref, r0, r1, nt + j)
                h_re[j], h_im[j] = n_re, n_im
                if keep:
                    st_ref[r0:r1, j * LANES:(j + 1) * LANES] = n_re
                    st_ref[r0:r1, (nt + j) * LANES:(nt + j + 1) * LANES] = n_im
            if r % pump_every == pump_every - 1:
                pump()
        return h_re, h_im

    h_re, h_im = scan([jnp.zeros((SUBLANES, LANES), F32) for _ in range(nt)],
                      [jnp.zeros((SUBLANES, LANES), F32) for _ in range(nt)], keep=False)

    for j in range(nt):
        q_re = lanes(lampow_ref, 1, 2, j)
        q_im = lanes(lampow_ref, 1, 2, nt + j)
        c_re = lanes(carry_ref, 0, 1, j)
        c_im = lanes(carry_ref, 0, 1, nt + j)
        for s in range(SUBLANES):
            cin_ref[s:s + 1, j * LANES:(j + 1) * LANES] = c_re
            cin_ref[s:s + 1, (nt + j) * LANES:(nt + j + 1) * LANES] = c_im
            e_re = h_re[j][s:s + 1, :]
            e_im = h_im[j][s:s + 1, :]
            c_re, c_im = (e_re + q_re * c_re - q_im * c_im,
                          e_im + q_re * c_im + q_im * c_re)
        carry_ref[0:1, j * LANES:(j + 1) * LANES] = c_re
        carry_ref[0:1, (nt + j) * LANES:(nt + j + 1) * LANES] = c_im

    scan([lanes(cin_ref, 0, SUBLANES, j) for j in range(nt)],
         [lanes(cin_ref, 0, SUBLANES, nt + j) for j in range(nt)], keep=True)

    half = tm // 2
    return jnp.concatenate([_dot(st_ref[0:half, :].astype(BF16), cblk_ref[...]),
                            _dot(st_ref[half:tm, :].astype(BF16), cblk_ref[...])], axis=0)


def _s5_finish(yp, u_perm, d_ref, wglu_ref, unperm_ref):
    y = yp + u_perm * d_ref[...]
    y = jax.nn.gelu(y)
    gate = jax.nn.sigmoid(_dot(y.astype(BF16), wglu_ref[...]))
    return _dot(unperm_ref[...], (y * gate).astype(BF16)).astype(BF16)


def _ssd_conv(cat_ref, cw_ref, cb_ref):
    halo = SUBLANES
    tm = cat_ref.shape[0] - halo
    acc = cb_ref[...] + cw_ref[SSD_CONV - 1:SSD_CONV, :] * cat_ref[halo:halo + tm, :]
    for k in range(1, SSD_CONV):
        acc = acc + (cw_ref[SSD_CONV - 1 - k:SSD_CONV - k, :]
                     * cat_ref[pl.ds(halo - k, tm), :])
    cat_ref[0:halo, :] = cat_ref[tm:tm + halo, :]
    return acc * jax.nn.sigmoid(acc)


def _ssd_body(xs, bc, dtv, alog_ref, dsk_ref, tri_ref, state_ref, pump, emit, *, n_heads):
    tm = xs.shape[0]
    q = SSD_CHUNK
    hd = SSD_HEAD_DIM
    hpg = n_heads // SSD_GROUPS
    gs = SSD_GROUPS * SSD_STATE

    a_neg = -jnp.exp(alog_ref[...])
    da = dtv * a_neg
    tri = tri_ref[...]
    li = lax.broadcasted_iota(jnp.int32, (q, q), 0)
    si = lax.broadcasted_iota(jnp.int32, (q, q), 1)
    causal = li >= si

    hpt = LANES // hd
    n_tiles = n_heads // hpt
    win = [(g * hpg * hd) // LANES for g in range(SSD_GROUPS)]
    assert all((g + 1) * hpg * hd <= (win[g] + 2) * LANES for g in range(SSD_GROUPS))
    first_head = lax.broadcasted_iota(jnp.int32, (q, LANES), 1) < hd
    first_head_row = lax.broadcasted_iota(jnp.int32, (1, LANES), 1) < hd
    win_head = lax.broadcasted_iota(jnp.int32, (q, 2 * LANES), 1) // hd

    def expand(src, rows, sel):
        return [jnp.where(sel,
                          jnp.broadcast_to(src[:, hpt * v:hpt * v + 1], (rows, LANES)),
                          jnp.broadcast_to(src[:, hpt * v + 1:hpt * v + 2], (rows, LANES)))
                for v in range(n_tiles)]

    def window(tiles, g):
        return jnp.concatenate([tiles[win[g]], tiles[win[g] + 1]], axis=1)

    n_chunks = tm // q
    groups = range(SSD_GROUPS)

    a_cs_all, a_cs_t_all = [], []
    for c in range(n_chunks):
        hi, mid, lo = _split3(da[c * q:(c + 1) * q, :])
        a_cs = _dot(tri, hi) + _dot(tri, mid) + _dot(tri, lo)
        a_cs_all.append(a_cs)
        a_cs_t_all.append(a_cs.T)

    for c in range(n_chunks):
        r0 = c * q
        a_cs, a_cs_t = a_cs_all[c], a_cs_t_all[c]
        dt_c = dtv[r0:r0 + q, :]
        dt_t = dt_c.T
        a_last = a_cs[q - 1:q, :]
        lanes_w = [slice(win[g] * LANES, (win[g] + 2) * LANES) for g in groups]
        x_w = [xs[r0:r0 + q, lanes_w[g]] for g in groups]
        x_bf = [x_w[g].astype(BF16) for g in groups]
        b_bf = [bc[r0:r0 + q, g * SSD_STATE:(g + 1) * SSD_STATE] for g in groups]
        c_bf = [bc[r0:r0 + q, gs + g * SSD_STATE:gs + (g + 1) * SSD_STATE] for g in groups]
        cb = [lax.dot_general(c_bf[g], b_bf[g], (((1,), (1,)), ((), ())),
                              preferred_element_type=F32) for g in groups]
        s_prev = [state_ref[g] for g in groups]
        y_off = [_dot(c_bf[g], s_prev[g].astype(BF16)) for g in groups]
        e_t = expand(jnp.exp(a_cs), q, first_head)
        w_t = expand(jnp.exp(a_last - a_cs) * dt_c, q, first_head)
        d_t = expand(jnp.exp(a_last), 1, first_head_row)
        m_all = []
        for h in range(n_heads):
            col = a_cs[:, h:h + 1]
            row = a_cs_t[h:h + 1, :]
            lmat = jnp.exp(jnp.where(causal, col - row, -jnp.inf))
            m_all.append((cb[h // hpg] * lmat * dt_t[h:h + 1, :]).astype(BF16))
            if h % hpg == hpg - 1 and (h // hpg) % 2 == 1:
                pump()
        y_win = []
        for g in groups:
            y_diag = None
            for j in range(hpg):
                h = g * hpg + j
                r_j = _dot(m_all[h], x_bf[g])
                y_diag = r_j if y_diag is None else jnp.where(
                    win_head == h - win[g] * hpt, r_j, y_diag)
            xw = (x_w[g] * window(w_t, g)).astype(BF16)
            upd = lax.dot_general(b_bf[g], xw, (((0,), (0,)), ((), ())),
                                  preferred_element_type=F32)
            state_ref[g] = s_prev[g] * window(d_t, g) + upd
            y_win.append(y_diag + y_off[g] * window(e_t, g) + x_w[g] * dsk_ref[0:1, lanes_w[g]])
        tiles = []
        for v in range(n_tiles):
            ga, gb = (hpt * v) // hpg, (hpt * v + 1) // hpg
            ta = y_win[ga][:, (v - win[ga]) * LANES:(v - win[ga] + 1) * LANES]
            if ga == gb:
                tiles.append(ta)
            else:
                tb = y_win[gb][:, (v - win[gb]) * LANES:(v - win[gb] + 1) * LANES]
                tiles.append(jnp.where(first_head, ta, tb))
        emit(c, jnp.concatenate(tiles, axis=1))


def _proj_items(h, win_ref, u_ref, cat_ref, z_ref, dt_ref, c_u, c_z, c_xbc, width):
    halo = SUBLANES
    tm = h.shape[0]

    def u_job():
        u = _dot(h, win_ref[:, 0:c_u])
        for w in range(c_u // LANES):
            u_ref[w] = u[:, w * LANES:(w + 1) * LANES]

    def col_job(dst_ref, r0, c_src, c_dst, n):
        def job():
            dst_ref[r0:r0 + tm, c_dst:c_dst + n] = _dot(h, win_ref[:, c_src:c_src + n])
        return job

    jobs = [u_job, col_job(dt_ref, 0, c_xbc, 0, LANES)]
    for c0 in range(0, c_z - c_u, width):
        jobs.append(col_job(z_ref, 0, c_u + c0, c0, min(width, c_z - c_u - c0)))
    n_early = len(jobs)
    for c0 in range(0, c_xbc - c_z, width):
        jobs.append(col_job(cat_ref, halo, c_z + c0, c0, min(width, c_xbc - c_z - c0)))
    return jobs, n_early


def _mixer_kernel(x_ref, xn_ref, nw_ref, win_ref,
                  bblk_ref, lampow_ref, cblk_ref, s5d_ref, wglu_ref,
                  cw_ref, cb_ref, dtb_ref, alog_ref, dsk_ref, snw_ref, tri_ref,
                  wo_ref, unperm_ref, o_ref,
                  u_ref, z_ref, dt_ref, st_ref, carry_ref, cin_ref, cat_ref, state_ref,
                  *, c_u, c_z, c_xbc, n_heads):
    halo = SUBLANES
    q = SSD_CHUNK
    proj = functools.partial(_proj_items, win_ref=win_ref, u_ref=u_ref, cat_ref=cat_ref,
                             z_ref=z_ref, dt_ref=dt_ref, c_u=c_u, c_z=c_z, c_xbc=c_xbc,
                             width=2 * LANES)

    @pl.when(pl.program_id(1) == 0)
    def _():
        carry_ref[...] = jnp.zeros_like(carry_ref)
        cat_ref[0:halo, :] = jnp.zeros((halo, cat_ref.shape[1]), F32)
        state_ref[...] = jnp.zeros_like(state_ref)
        for job in proj(_rms(x_ref[...], nw_ref[...]).astype(BF16))[0]:
            job()

    u_perm = _s5_gather(u_ref)
    zz = z_ref[...]
    z_gate = zz * jax.nn.sigmoid(zz)
    dtx = dt_ref[...] + dtb_ref[...]
    dtv = jnp.maximum(dtx, 0.0) + jnp.log(1.0 + jnp.exp(-jnp.abs(dtx)))

    jobs, n_early = proj(_rms(xn_ref[...], nw_ref[...]).astype(BF16))
    early, late = jobs[:n_early], jobs[n_early:]

    def pump_early():
        if early:
            early.pop(0)()

    def pump_late():
        if late:
            late.pop(0)()

    yp = _s5_scan(u_perm, bblk_ref, lampow_ref, cblk_ref, st_ref, carry_ref, cin_ref,
                  pump_early, pump_every=32)
    xbc = _ssd_conv(cat_ref, cw_ref, cb_ref)
    while early:
        pump_early()
    y_s5 = _s5_finish(yp, u_perm, s5d_ref, wglu_ref, unperm_ref)
    o_ref[...] = x_ref[...] + _dot(y_s5, wo_ref[0:c_u, :])

    def emit(c, y_c):
        rows = slice(c * q, (c + 1) * q)
        y_n = _rms(y_c * z_gate[rows, :], snw_ref[...]).astype(BF16)
        o_ref[rows, :] = o_ref[rows, :] + _dot(y_n, wo_ref[c_u:c_z, :])

    inner = c_z - c_u
    _ssd_body(xbc[:, 0:inner], xbc[:, inner:].astype(BF16), dtv, alog_ref, dsk_ref, tri_ref,
              state_ref, pump_late, emit, n_heads=n_heads)
    while late:
        pump_late()


def _mixer(x3, layer, norm_w, w_in_p, s5p, ssdp, w_out, c_u, c_z, c_xbc, n_heads, tm):
    b, l, d = x3.shape
    ns2 = s5p[0].shape[2]
    inner = c_z - c_u
    cdim = c_xbc - c_z
    n_t = l // tm
    consts = (norm_w, w_in_p) + tuple(s5p) + tuple(ssdp) + (w_out,)
    tok = jnp.arange(tm)
    seg_len = tm // SUBLANES
    unperm = (tok[None, :] == (SUBLANES * (tok % seg_len) + tok // seg_len)[:, None])
    unperm = unperm.astype(BF16)[None]
    kern = functools.partial(_mixer_kernel, c_u=c_u, c_z=c_z, c_xbc=c_xbc, n_heads=n_heads)
    blk = pl.BlockSpec((None, tm, d), lambda bi, i: (bi, i, 0))
    nxt = pl.BlockSpec((None, tm, d), lambda bi, i: (bi, jnp.minimum(i + 1, n_t - 1), 0))
    return pl.pallas_call(
        kern,
        grid=(b, n_t),
        in_specs=[blk, nxt] + [_layer_spec(a, layer) for a in consts] + [_layer_spec(unperm, 0)],
        out_specs=blk,
        out_shape=jax.ShapeDtypeStruct((b, l, d), F32),
        scratch_shapes=[pltpu.VMEM((c_u // LANES, tm, LANES), F32),
                        pltpu.VMEM((tm, inner), F32),
                        pltpu.VMEM((tm, LANES), F32),
                        pltpu.VMEM((tm, ns2), F32),
                        pltpu.VMEM((SUBLANES, ns2), F32),
                        pltpu.VMEM((SUBLANES, ns2), F32),
                        pltpu.VMEM((tm + SUBLANES, cdim), F32),
                        pltpu.VMEM((SSD_GROUPS, SSD_STATE, 2 * LANES), F32)],
        compiler_params=_cparams(2, True),
        name="mixer",
    )(x3, x3, *consts, unperm)


def _kv_kernel(mem_ref, nw_ref, wk_ref, wv_ref, wq_ref, wo_ref, ws_ref, wvo_ref):
    m = _rms(mem_ref[...], nw_ref[...]).astype(BF16)
    n_mem, d = mem_ref.shape
    k = _dot(m, wk_ref[...]).astype(BF16)
    v = _dot(m, wv_ref[...]).astype(BF16)
    hd = d // XA_HEADS
    scale = hd ** -0.5
    for hh in range(XA_HEADS):
        cols = slice(hh * hd, (hh + 1) * hd)
        mcols = slice(hh * n_mem, (hh + 1) * n_mem)
        qk = lax.dot_general(wq_ref[:, cols], k[:, cols], (((1,), (1,)), ((), ())),
                             preferred_element_type=F32)
        ws_ref[:, mcols] = (qk * scale).astype(ws_ref.dtype)
        wvo_ref[mcols, :] = _dot(v[:, cols], wo_ref[cols, :]).astype(wvo_ref.dtype)


def _mem_kv(mem, mem_norm_w, wk, wv, wq, wo):
    b, m, d = mem.shape
    nl = wk.shape[0]
    hm = XA_HEADS * m
    wspec = pl.BlockSpec((None, d, d), lambda li, bi: (li, 0, 0))
    return pl.pallas_call(
        _kv_kernel,
        grid=(nl, b),
        in_specs=[pl.BlockSpec((None, m, d), lambda li, bi: (bi, 0, 0)),
                  pl.BlockSpec((None, 1, d), lambda li, bi: (li, 0, 0)),
                  wspec, wspec, wspec, wspec],
        out_specs=[pl.BlockSpec((None, None, d, hm), lambda li, bi: (li, bi, 0, 0)),
                   pl.BlockSpec((None, None, hm, d), lambda li, bi: (li, bi, 0, 0))],
        out_shape=[jax.ShapeDtypeStruct((nl, b, d, hm), BF16),
                   jax.ShapeDtypeStruct((nl, b, hm, d), BF16)],
        compiler_params=_cparams(2, False),
        name="mem_kv",
    )(mem, mem_norm_w, wk, wv, wq, wo)


def _xattn_kernel(x_ref, nw_ref, ws_ref, wvo_ref, o_ref, p_ref):
    x = x_ref[...]
    h = _rms(x, nw_ref[...]).astype(BF16)
    s_all = _dot(h, ws_ref[...])
    n_mem = ws_ref.shape[1] // XA_HEADS
    for hh in range(XA_HEADS):
        mcols = slice(hh * n_mem, (hh + 1) * n_mem)
        s = s_all[:, mcols]
        s = s - jnp.max(s, axis=-1, keepdims=True)
        p = jnp.exp(s)
        p = p / jnp.sum(p, axis=-1, keepdims=True)
        p_ref[:, mcols] = p.astype(p_ref.dtype)
    o_ref[...] = x + _dot(p_ref[...], wvo_ref[...])


def _xattn(x3, layer, norm_w, ws, wvo, tm):
    b, l, d = x3.shape
    hm = ws.shape[3]
    blk = pl.BlockSpec((None, tm, d), lambda bi, i: (bi, i, 0))
    return pl.pallas_call(
        _xattn_kernel,
        grid=(b, l // tm),
        in_specs=[blk, _layer_spec(norm_w, layer),
                  pl.BlockSpec((None, None, d, hm), lambda bi, i: (layer, bi, 0, 0)),
                  pl.BlockSpec((None, None, hm, d), lambda bi, i: (layer, bi, 0, 0))],
        out_specs=blk,
        out_shape=jax.ShapeDtypeStruct((b, l, d), F32),
        scratch_shapes=[pltpu.VMEM((tm, hm), BF16)],
        compiler_params=_cparams(2, False),
        name="xattn",
    )(x3, norm_w, ws, wvo)


def _ffn_kernel(x_ref, xn_ref, nw_ref, wup_ref, cw_ref, cb_ref, wdn_ref, fnw_ref, o_ref,
                h_ref, halo_ref, act_ref, *, d_ff, chunk, final_norm):
    n_sub, tm, _ = h_ref.shape

    @pl.when(pl.program_id(1) == 0)
    def _():
        halo_ref[...] = jnp.zeros_like(halo_ref)
        h_ref[0] = _rms(x_ref[0:tm, :], nw_ref[...]).astype(BF16)

    row = lax.broadcasted_iota(jnp.int32, (tm, chunk), 0)
    n_chunks = d_ff // chunk
    for k in range(n_sub):
        rows = slice(k * tm, (k + 1) * tm)
        h = h_ref[k]
        for c in range(n_chunks):
            cols = slice(c * chunk, (c + 1) * chunk)
            g = _dot(h, wup_ref[:, cols])
            v = _dot(h, wup_ref[:, d_ff + c * chunk:d_ff + (c + 1) * chunk])
            p1 = halo_ref[SUBLANES - 1:SUBLANES, cols]
            p2 = halo_ref[SUBLANES - 2:SUBLANES - 1, cols]
            gm1 = jnp.where(row == 0, p1, pltpu.roll(g, 1, 0))
            gm2 = jnp.where(row == 0, p2, jnp.where(row == 1, p1, pltpu.roll(g, 2, 0)))
            halo_ref[:, cols] = g[tm - SUBLANES:tm, :]
            y = (cb_ref[0:1, cols] + cw_ref[0:1, cols] * gm2 + cw_ref[1:2, cols] * gm1
                 + cw_ref[2:3, cols] * g)
            act_ref[:, cols] = (y * jax.nn.sigmoid(y) * v).astype(BF16)
            if c == n_chunks // 2:
                nxt = x_ref[(k + 1) * tm:(k + 2) * tm, :] if k + 1 < n_sub else xn_ref[...]
                h_ref[(k + 1) % n_sub] = _rms(nxt, nw_ref[...]).astype(BF16)
        out = x_ref[rows, :] + _dot(act_ref[...], wdn_ref[...])
        if final_norm:
            out = _rms(out, fnw_ref[...])
        o_ref[rows, :] = out


def _ffn(x3, layer, norm_w, w_up, conv_w, conv_b, w_down, final_w, final_norm, tm, chunk):
    b, l, d = x3.shape
    d_ff = w_down.shape[1]
    kern = functools.partial(_ffn_kernel, d_ff=d_ff, chunk=chunk, final_norm=final_norm)
    n_t = l // tm
    n_sub = FFN_SUB
    blk = pl.BlockSpec((None, n_sub * tm, d), lambda bi, i: (bi, i, 0))
    nxt = pl.BlockSpec((None, tm, d),
                       lambda bi, i: (bi, jnp.minimum(n_sub * (i + 1), n_t - 1), 0))
    return pl.pallas_call(
        kern,
        grid=(b, n_t // n_sub),
        in_specs=[blk, nxt]
        + [_layer_spec(a, layer) for a in (norm_w, w_up, conv_w, conv_b, w_down)]
        + [_layer_spec(final_w, 0)],
        out_specs=blk,
        out_shape=jax.ShapeDtypeStruct((b, l, d), F32),
        scratch_shapes=[pltpu.VMEM((n_sub, tm, d), BF16),
                        pltpu.VMEM((SUBLANES, d_ff), F32),
                        pltpu.VMEM((tm, d_ff), BF16)],
        compiler_params=_cparams(2, True),
        name="conv_ffn",
    )(x3, x3, norm_w, w_up, conv_w, conv_b, w_down, final_w)


def _s5_params(lam_re, lam_im, log_dt, b_re, b_im, c_re, c_im, seg_len):
    g, p = lam_re.shape
    cg = b_re.shape[2]
    lr = jnp.minimum(lam_re.astype(F32), -1e-4)
    li = lam_im.astype(F32)
    dt = jnp.exp(log_dt.astype(F32))[:, None]
    mag = jnp.exp(lr * dt)
    ab_re = mag * jnp.cos(li * dt)
    ab_im = mag * jnp.sin(li * dt)
    den = lr * lr + li * li
    f_re = ((ab_re - 1.0) * lr + ab_im * li) / den
    f_im = (ab_im * lr - (ab_re - 1.0) * li) / den
    br = b_re.astype(F32)
    bi = b_im.astype(F32)
    bb_re = f_re[..., None] * br - f_im[..., None] * bi
    bb_im = f_re[..., None] * bi + f_im[..., None] * br
    eye = jnp.eye(g, dtype=F32)

    def in_blk(bb):
        return jnp.einsum('gpc,gh->gchp', bb, eye).reshape(g * cg, g * p)

    def out_blk(cc):
        return jnp.einsum('gcp,gh->gphc', cc, eye).reshape(g * p, g * cg)

    bblk = jnp.concatenate([in_blk(bb_re), in_blk(bb_im)], axis=1).astype(BF16)
    cblk = jnp.concatenate([out_blk(c_re.astype(F32)), -out_blk(c_im.astype(F32))],
                           axis=0).astype(BF16)
    seg_mag = jnp.exp(seg_len * (lr * dt))
    seg_ang = seg_len * (li * dt)
    lampow = jnp.stack([
        jnp.concatenate([ab_re.reshape(-1), ab_im.reshape(-1)]),
        jnp.concatenate([(seg_mag * jnp.cos(seg_ang)).reshape(-1),
                         (seg_mag * jnp.sin(seg_ang)).reshape(-1)])])
    return bblk, lampow, cblk


def _rows(v, width=None):
    v = v.astype(F32)[:, None, :]
    if width is not None:
        v = jnp.pad(v, ((0, 0), (0, 0), (0, width - v.shape[2])))
    return v


def kernel(x, mem, mix_norm_w, w_in, s5_lambda_re, s5_lambda_im, s5_log_dt, s5_b_re, s5_b_im,
           s5_c_re, s5_c_im, s5_d, s5_w_glu, ssd_conv_w, ssd_conv_b, ssd_dt_bias, ssd_a_log,
           ssd_d, ssd_norm_w, w_out, xa_norm_w, mem_norm_w, xa_wq, xa_wk, xa_wv, xa_wo,
           ffn_norm_w, ffn_w_up, ffn_conv_w, ffn_conv_b, ffn_w_down, final_norm_w):
    bsz, seq, d = x.shape
    depth = w_in.shape[0]
    s5_w = s5_d.shape[1]
    inner = ssd_norm_w.shape[1]
    n_heads = ssd_a_log.shape[1]
    cdim = ssd_conv_w.shape[2]
    c_u, c_z, c_xbc = s5_w, s5_w + inner, s5_w + inner + cdim
    in_cols = w_in.shape[2]

    tri = jnp.tril(jnp.ones((1, SSD_CHUNK, SSD_CHUNK), F32)).astype(BF16)
    tri = jnp.broadcast_to(tri, (depth, SSD_CHUNK, SSD_CHUNK))
    w_in_p = jnp.pad(w_in, ((0, 0), (0, 0), (0, c_xbc + LANES - in_cols))).astype(BF16)
    bblk, lampow, cblk = jax.vmap(functools.partial(_s5_params, seg_len=TM_MIX // SUBLANES))(
        s5_lambda_re, s5_lambda_im, s5_log_dt, s5_b_re, s5_b_im, s5_c_re, s5_c_im)
    s5p = (bblk, lampow, cblk, _rows(s5_d), s5_w_glu.astype(BF16))
    ssdp = (ssd_conv_w.astype(F32), _rows(ssd_conv_b), _rows(ssd_dt_bias, LANES),
            _rows(ssd_a_log, LANES), _rows(jnp.repeat(ssd_d, SSD_HEAD_DIM, axis=1)),
            _rows(ssd_norm_w), tri)
    w_out_bf = w_out.astype(BF16)
    mix_nw, xa_nw, ffn_nw = _rows(mix_norm_w), _rows(xa_norm_w), _rows(ffn_norm_w)
    w_up_bf, w_down_bf = ffn_w_up.astype(BF16), ffn_w_down.astype(BF16)
    ffn_cw, ffn_cb = ffn_conv_w.astype(F32), _rows(ffn_conv_b)
    final_w = final_norm_w.astype(F32).reshape(1, 1, d)
    ws_all, wvo_all = _mem_kv(mem, _rows(mem_norm_w), xa_wk.astype(BF16),
                              xa_wv.astype(BF16), xa_wq.astype(BF16), xa_wo.astype(BF16))

    for i in range(depth):
        x = _mixer(x, i, mix_nw, w_in_p, s5p, ssdp, w_out_bf, c_u, c_z, c_xbc, n_heads, TM_MIX)
        x = _xattn(x, i, xa_nw, ws_all, wvo_all, TM_XA)
        x = _ffn(x, i, ffn_nw, w_up_bf, ffn_cw, ffn_cb, w_down_bf, final_w,
                 i == depth - 1, TM_FFN, FFN_CHUNK)
    return x
```
